```python
import jax, jax.numpy as jnp
from jax import lax
import numpy as np

D_MODEL = 2048
BATCH = 4
SEQ = 4096
DEPTH = 1

CHUNK = 64
N_META = 16
EPS = 1e-6
RG_WIDTH = 2048
RG_HEADS = 16
RG_HEAD_DIM = RG_WIDTH // RG_HEADS
CONV_WIDTH = 4
RG_C = 8.0
HG_HEADS = 16
HG_EXPAND = 128
HG_HEAD_DIM = 128
HG_WIDTH = HG_HEADS * HG_EXPAND
PEER_HEADS = 8
N_KEYS = 128
N_EXPERTS = N_KEYS * N_KEYS
PEER_TOPK = 16
D_QUERY = 256
PEER_BLOCK = 128

IN_SPLITS = (RG_WIDTH, 2 * RG_WIDTH, 2 * RG_WIDTH + HG_WIDTH, 2 * RG_WIDTH + 2 * HG_WIDTH,
             2 * RG_WIDTH + 3 * HG_WIDTH, 2 * RG_WIDTH + 4 * HG_WIDTH,
             2 * RG_WIDTH + 4 * HG_WIDTH + D_MODEL)
IN_WIDTH = 2 * RG_WIDTH + 4 * HG_WIDTH + 2 * D_MODEL

kernel_name = "hybrid_rglru_hgrn2_peer_block"


def rms_norm(x, g):
    xf = x.astype(jnp.float32)
    y = xf * lax.rsqrt(jnp.mean(xf * xf, axis=-1, keepdims=True) + EPS)
    return (y * g.astype(jnp.float32)).astype(x.dtype)


def causal_dwconv(x, w, b):
    T = x.shape[1]
    xp = jnp.pad(x, ((0, 0), (CONV_WIDTH - 1, 0), (0, 0)))
    y = b
    for k in range(CONV_WIDTH):
        y = y + xp[:, k:k + T] * w[k]
    return y


def rg_lru(x, wa, ba, wx, bx, lam):
    B, T, _ = x.shape
    xf = x.astype(jnp.float32)
    xh = xf.reshape(B, T, RG_HEADS, RG_HEAD_DIM)
    r = jax.nn.sigmoid(jnp.einsum('bthi,hij->bthj', xh, wa.astype(jnp.float32)).reshape(B, T, RG_WIDTH) + ba)
    i = jax.nn.sigmoid(jnp.einsum('bthi,hij->bthj', xh, wx.astype(jnp.float32)).reshape(B, T, RG_WIDTH) + bx)
    log_a = -RG_C * r * jax.nn.softplus(-lam.astype(jnp.float32))
    a = jnp.exp(log_a)
    u = jnp.sqrt(-jnp.expm1(2.0 * log_a)) * (i * xf)

    def combine(e, l):
        return l[0] * e[0], l[0] * e[1] + l[1]

    _, h = lax.associative_scan(combine, (a, u), axis=1)
    return h.astype(x.dtype)


def hgrn2(q, f_logits, v, og, lb, norm_g):
    B, T, _ = q.shape
    dt = q.dtype
    f = lb + (1.0 - lb) * jax.nn.sigmoid(f_logits.astype(jnp.float32))
    log_f = jnp.log(f)
    k = 1.0 - f
    qs = jax.nn.silu(q.astype(jnp.float32))
    vf = v.astype(jnp.float32)
    pad = (-T) % CHUNK
    Tp = T + pad
    nc = Tp // CHUNK

    def to_chunks(t, d):
        t = jnp.pad(t, ((0, 0), (pad, 0), (0, 0)))
        return t.reshape(B, nc, CHUNK, HG_HEADS, d).transpose(1, 0, 3, 2, 4)

    qc, kc, lfc = to_chunks(qs, HG_EXPAND), to_chunks(k, HG_EXPAND), to_chunks(log_f, HG_EXPAND)
    vc = to_chunks(vf, HG_HEAD_DIM)
    causal = jnp.tril(jnp.ones((CHUNK, CHUNK), dtype=bool))

    def step(S, inp):
        qt, kt, vt, lf = inp
        bcum = jnp.cumsum(lf, axis=2)
        diff = bcum[:, :, :, None, :] - bcum[:, :, None, :, :]
        decay = jnp.exp(jnp.where(causal[:, :, None], diff, -jnp.inf))
        scores = jnp.einsum('bhtk,bhtsk,bhsk->bhts', qt, decay, kt)
        o = (jnp.einsum('bhts,bhsv->bhtv', scores, vt)
             + jnp.einsum('bhtk,bhkv->bhtv', qt * jnp.exp(bcum), S))
        blast = bcum[:, :, -1:, :]
        S = (jnp.exp(blast[:, :, 0, :])[..., None] * S
             + jnp.einsum('bhsk,bhsv->bhkv', kt * jnp.exp(blast - bcum), vt))
        return S, o

    S0 = jnp.zeros((B, HG_HEADS, HG_EXPAND, HG_HEAD_DIM), jnp.float32)
    _, o = lax.scan(step, S0, (qc, kc, vc, lfc))
    o = o.transpose(1, 0, 3, 2, 4).reshape(B, Tp, HG_HEADS, HG_HEAD_DIM)[:, pad:]
    o = o * lax.rsqrt(jnp.mean(o * o, axis=-1, keepdims=True) + EPS)
    o = o.reshape(B, T, HG_HEADS * HG_HEAD_DIM) * norm_g.astype(jnp.float32)
    return (o * jax.nn.silu(og.astype(jnp.float32))).astype(dt)


def peer(x, wq, keys, u, v):
    B, T, D = x.shape
    n = B * T
    xt = x.reshape(n, D)
    q = (xt @ wq).reshape(n, PEER_HEADS, 2, D_QUERY // 2)
    s = jnp.einsum('nhpd,hpkd->nhpk', q, keys).astype(jnp.float32)
    s_top, i_top = lax.top_k(s, PEER_TOPK)
    cand = (s_top[:, :, 0, :, None] + s_top[:, :, 1, None, :]).reshape(n, PEER_HEADS, PEER_TOPK * PEER_TOPK)
    cand_idx = (i_top[:, :, 0, :, None] * N_KEYS + i_top[:, :, 1, None, :]).reshape(n, PEER_HEADS, PEER_TOPK * PEER_TOPK)
    best, pos = lax.top_k(cand, PEER_TOPK)
    idx = jnp.take_along_axis(cand_idx, pos, axis=-1).reshape(n, PEER_HEADS * PEER_TOPK)
    gate = jax.nn.softmax(best, axis=-1).reshape(n, PEER_HEADS * PEER_TOPK).astype(x.dtype)
    pad_n = (-n) % PEER_BLOCK
    nb = (n + pad_n) // PEER_BLOCK
    xb = jnp.pad(xt, ((0, pad_n), (0, 0))).reshape(nb, PEER_BLOCK, D)
    ib = jnp.pad(idx, ((0, pad_n), (0, 0))).reshape(nb, PEER_BLOCK, -1)
    gb = jnp.pad(gate, ((0, pad_n), (0, 0))).reshape(nb, PEER_BLOCK, -1)

    def block(args):
        xs, ids, gs = args
        ue = jnp.take(u, ids, axis=0)
        act = jax.nn.gelu(jnp.einsum('nd,ned->ne', xs, ue), approximate=False)
        ve = jnp.take(v, ids, axis=0)
        return jnp.einsum('ne,ned->nd', gs * act, ve)

    out = lax.map(block, (xb, ib, gb)).reshape(nb * PEER_BLOCK, D)[:n]
    return out.reshape(B, T, D)


def setup_inputs(seed: int = 0) -> dict:
    key = jax.random.key(seed)
    ks = jax.random.split(key, 24)
    f32 = jnp.float32
    L = DEPTH

    def nrm(k, shape, scale):
        return jax.random.normal(k, shape, f32) * scale

    a_c = jax.random.uniform(ks[11], (L, RG_WIDTH), f32, 0.9, 0.999)
    a = a_c ** (1.0 / RG_C)
    rg_lambda = jnp.log(a) - jnp.log1p(-a)
    return {
        'x': nrm(ks[0], (BATCH, SEQ, D_MODEL), 1.0),
        'meta': nrm(ks[1], (N_META, D_MODEL), 1.0),
        'ln1_g': 1.0 + nrm(ks[2], (L, D_MODEL), 0.02),
        'w_in': nrm(ks[3], (L, D_MODEL, IN_WIDTH), D_MODEL ** -0.5),
        'conv_w': nrm(ks[4], (L, CONV_WIDTH, RG_WIDTH), CONV_WIDTH ** -0.5),
        'conv_b': nrm(ks[5], (L, RG_WIDTH), 0.02),
        'rg_wa': nrm(ks[6], (L, RG_HEADS, RG_HEAD_DIM, RG_HEAD_DIM), RG_HEAD_DIM ** -0.5),
        'rg_ba': nrm(ks[7], (L, RG_WIDTH), 0.02),
        'rg_wx': nrm(ks[8], (L, RG_HEADS, RG_HEAD_DIM, RG_HEAD_DIM), RG_HEAD_DIM ** -0.5),
        'rg_bx': nrm(ks[9], (L, RG_WIDTH), 0.02),
        'rg_lambda': rg_lambda,
        'hg_lb_logits': nrm(ks[10], (L + 1, HG_WIDTH), 0.5),
        'hg_norm_g': 1.0 + nrm(ks[12], (L, HG_WIDTH), 0.02),
        'w_pa': nrm(ks[13], (L, RG_WIDTH, D_MODEL), RG_WIDTH ** -0.5),
        'w_pb': nrm(ks[14], (L, HG_WIDTH, D_MODEL), HG_WIDTH ** -0.5),
        'w_out': nrm(ks[15], (L, D_MODEL, D_MODEL), D_MODEL ** -0.5),
        'ln2_g': 1.0 + nrm(ks[16], (L, D_MODEL), 0.02),
        'peer_wq': nrm(ks[17], (L, D_MODEL, PEER_HEADS * D_QUERY), D_MODEL ** -0.5),
        'peer_keys': nrm(ks[18], (L, PEER_HEADS, 2, N_KEYS, D_QUERY // 2), (D_QUERY // 2) ** -0.5),
        'peer_u': nrm(ks[19], (L, N_EXPERTS, D_MODEL), D_MODEL ** -0.5),
        'peer_v': nrm(ks[20], (L, N_EXPERTS, D_MODEL), PEER_HEADS ** -0.5),
        'final_g': 1.0 + nrm(ks[21], (D_MODEL,), 0.02),
    }


def reference(x, meta, ln1_g, w_in, conv_w, conv_b, rg_wa, rg_ba, rg_wx, rg_bx, rg_lambda,
              hg_lb_logits, hg_norm_g, w_pa, w_pb, w_out, ln2_g, peer_wq, peer_keys,
              peer_u, peer_v, final_g):
    B = x.shape[0]
    h = jnp.concatenate([jnp.broadcast_to(meta[None].astype(x.dtype), (B, N_META, D_MODEL)), x], axis=1)
    lb_all = jnp.cumsum(jax.nn.softmax(hg_lb_logits.astype(jnp.float32), axis=0), axis=0)
    for l in range(DEPTH):
        xn = rms_norm(h, ln1_g[l])
        proj = xn @ w_in[l]
        xa, ya, qb, fb, ib, gb, za, zb = jnp.split(proj, IN_SPLITS, axis=-1)
        ha = rg_lru(causal_dwconv(xa, conv_w[l], conv_b[l]), rg_wa[l], rg_ba[l], rg_wx[l], rg_bx[l], rg_lambda[l])
        y_a = jax.nn.gelu(ya, approximate=False) * ha
        y_b = hgrn2(qb, fb, ib, gb, lb_all[l], hg_norm_g[l])
        mixed = jax.nn.sigmoid(za) * (y_a @ w_pa[l]) + jax.nn.sigmoid(zb) * (y_b @ w_pb[l])
        h = h + mixed @ w_out[l]
        h = h + peer(rms_norm(h, ln2_g[l]), peer_wq[l], peer_keys[l], peer_u[l], peer_v[l])
    h = rms_norm(h, final_g)
    return h[:, N_META:]
```

```python
import functools
import math

import jax
import jax.numpy as jnp
from jax import lax
from jax.experimental import pallas as pl
from jax.experimental.pallas import tpu as pltpu

F32 = jnp.float32
BF16 = jnp.bfloat16

EPS = 1e-6
RG_C = 8.0
CONV_WIDTH = 4
HEAD = 128
N_KEYS = 128
PEER_TOPK = 16
TT = 128
RB = 32
VMEM_LIMIT = 56 * 1024 * 1024

NT_DIMS = (((1,), (1,)), ((), ()))
TN_DIMS = (((0,), (0,)), ((), ()))


def _params(sem):
    return pltpu.CompilerParams(dimension_semantics=sem, vmem_limit_bytes=VMEM_LIMIT)


def _sigmoid(x):
    return jax.nn.sigmoid(x)


def _gelu(x):
    return 0.5 * x * (1.0 + lax.erf(x * (1.0 / math.sqrt(2.0))))


def _norm_mm_kernel(x_ref, g_ref, w_ref, *rest):
    o_ref, xn_ref = rest[-2], rest[-1]

    @pl.when(pl.program_id(1) == 0)
    def _():
        xf = x_ref[...]
        ms = jnp.mean(xf * xf, axis=-1, keepdims=True)
        xn_ref[...] = (xf * lax.rsqrt(ms + EPS) * g_ref[...]).astype(BF16)

    o_ref[...] = jnp.dot(xn_ref[...], w_ref[...], preferred_element_type=F32).astype(o_ref.dtype)


def _norm_matmul(x, g, w, out_rows, row_block_offset, tm, tn, prev=None):
    m, d = x.shape
    n = w.shape[1]
    in_specs = [
        pl.BlockSpec((tm, d), lambda i, j: (i, 0)),
        pl.BlockSpec((1, d), lambda i, j: (0, 0)),
        pl.BlockSpec((d, tn), lambda i, j: (0, j)),
    ]
    args = [x, g, w]
    aliases = {}
    if prev is not None:
        in_specs.append(pl.BlockSpec(memory_space=pl.ANY))
        args.append(prev)
        aliases = {3: 0}
    return pl.pallas_call(
        _norm_mm_kernel,
        grid=(m // tm, n // tn),
        in_specs=in_specs,
        out_specs=pl.BlockSpec((tm, tn), lambda i, j: (i + row_block_offset, j)),
        out_shape=jax.ShapeDtypeStruct((out_rows, n), BF16),
        scratch_shapes=[pltpu.VMEM((tm, d), BF16)],
        input_output_aliases=aliases,
        compiler_params=_params(("parallel", "arbitrary")),
        name="norm_matmul",
    )(*args)


def _rglru_kernel(n_pad, xa_ref, ya_ref, cw_ref, cb_ref, wa_ref, ba_ref, wx_ref, bx_ref, lam_ref,
                  o_ref, xbuf_ref, hc_ref):
    t = pl.program_id(1)
    width = xa_ref.shape[1]

    @pl.when(t == 0)
    def _():
        xbuf_ref[0:8, :] = jnp.zeros((8, width), F32)
        hc_ref[...] = jnp.zeros_like(hc_ref)

    xbuf_ref[8:8 + TT, :] = xa_ref[...].astype(F32)
    row = lax.broadcasted_iota(jnp.int32, (TT, HEAD), 0)
    live = jnp.logical_or(t > 0, row >= n_pad)
    r8 = row & 7

    for h in range(width // HEAD):
        sl = slice(h * HEAD, (h + 1) * HEAD)
        xc = cb_ref[:, sl] + cw_ref[0:1, sl] * xbuf_ref[5:5 + TT, sl]
        for k in range(1, CONV_WIDTH):
            xc = xc + cw_ref[k:k + 1, sl] * xbuf_ref[5 + k:5 + k + TT, sl]
        xcb = xc.astype(BF16)
        r = _sigmoid(jnp.dot(xcb, wa_ref[h], preferred_element_type=F32) + ba_ref[:, sl])
        i = _sigmoid(jnp.dot(xcb, wx_ref[h], preferred_element_type=F32) + bx_ref[:, sl])
        sp = jax.nn.softplus(-lam_ref[:, sl])
        log_a = (-RG_C) * r * sp
        a = jnp.exp(log_a)
        th = jnp.tanh(log_a)
        u = jnp.sqrt(-2.0 * th / (1.0 - th)) * (i * xc)
        u = jnp.where(live, u, 0.0)
        for k in (1, 2, 4):
            a_sh = pltpu.roll(a, k, 0)
            u_sh = pltpu.roll(u, k, 0)
            m = r8 >= k
            u = jnp.where(m, a * u_sh + u, u)
            a = jnp.where(m, a * a_sh, a)
        carry = hc_ref[:, sl]
        outs = []
        for g in range(TT // 8):
            hg = a[g * 8:(g + 1) * 8] * carry + u[g * 8:(g + 1) * 8]
            outs.append(hg)
            carry = hg[7:8]
        hc_ref[:, sl] = carry
        hs = jnp.concatenate(outs, axis=0)
        o_ref[:, sl] = (_gelu(ya_ref[:, sl].astype(F32)) * hs).astype(o_ref.dtype)

    xbuf_ref[0:8, :] = xbuf_ref[TT:TT + 8, :]


def _rglru(proj, batch, seq, n_pad, conv_w, conv_b, wa, ba, wx, bx, lam):
    n = batch * seq
    width = conv_w.shape[1]
    tiles = seq // TT
    meta_blk = n // TT

    def in_row(b, t):
        return jnp.where(t == 0, meta_blk, b * tiles + t - 1)

    vec = lambda: pl.BlockSpec((1, width), lambda b, t: (0, 0))
    gate_w = lambda: pl.BlockSpec((width // HEAD, HEAD, HEAD), lambda b, t: (0, 0, 0))
    return pl.pallas_call(
        functools.partial(_rglru_kernel, n_pad),
        grid=(batch, tiles + 1),
        in_specs=[
            pl.BlockSpec((TT, width), lambda b, t: (in_row(b, t), 0)),
            pl.BlockSpec((TT, width), lambda b, t: (in_row(b, t), 1)),
            pl.BlockSpec((CONV_WIDTH, width), lambda b, t: (0, 0)),
            vec(), gate_w(), vec(), gate_w(), vec(), vec(),
        ],
        out_specs=pl.BlockSpec((TT, width), lambda b, t: (b * tiles + jnp.maximum(t - 1, 0), 0)),
        out_shape=jax.ShapeDtypeStruct((n, width), BF16),
        scratch_shapes=[pltpu.VMEM((TT + 8, width), F32), pltpu.VMEM((1, width), F32)],
        compiler_params=_params(("parallel", "arbitrary")),
        name="rglru",
    )(proj, proj, conv_w, conv_b, wa, ba, wx, bx, lam)


def _hgrn_kernel(n_pad, q_ref, f_ref, v_ref, g_ref, lbl_ref, ng_ref, o_ref, st_ref):
    t = pl.program_id(1)
    width = q_ref.shape[1]

    @pl.when(t == 0)
    def _():
        st_ref[...] = jnp.zeros_like(st_ref)

    row = lax.broadcasted_iota(jnp.int32, (TT, HEAD), 0)
    live = jnp.logical_or(t > 0, row >= n_pad)
    r8 = row & 7
    tri = (lax.broadcasted_iota(jnp.int32, (RB, RB), 0) >= lax.broadcasted_iota(jnp.int32, (RB, RB), 1))

    for h in range(width // HEAD):
        sl = slice(h * HEAD, (h + 1) * HEAD)
        lg = lbl_ref[:, sl]
        e = jnp.exp(lg - jnp.max(lg, axis=0, keepdims=True))
        lb = e[0:1] / jnp.sum(e, axis=0, keepdims=True)
        f = lb + (1.0 - lb) * _sigmoid(f_ref[:, sl].astype(F32))
        lf = jnp.where(live, jnp.log(f), 0.0)
        kk = 1.0 - f
        q = q_ref[:, sl].astype(F32)
        qs = q * _sigmoid(q)
        vb = v_ref[:, sl]

        c = lf
        for k in (1, 2, 4):
            c = c + jnp.where(r8 >= k, pltpu.roll(c, k, 0), 0.0)
        parts = []
        off = jnp.zeros((1, HEAD), F32)
        for g in range(TT // 8):
            cg = c[g * 8:(g + 1) * 8] + off
            parts.append(cg)
            off = cg[7:8]
        bcum = jnp.concatenate(parts, axis=0)
        bend = off

        st = st_ref[h]
        qhat = (qs * jnp.exp(bcum)).astype(BF16)
        o_inter = lax.dot_general(qhat, st.astype(BF16), NT_DIMS, preferred_element_type=F32)

        outs = []
        for blk in range(TT // RB):
            r0 = blk * RB
            nk = r0 + RB
            mid = r0 + RB // 2 - 1
            bref = bcum[mid:mid + 1]
            qt = (qs[r0:nk] * jnp.exp(bcum[r0:nk] - bref)).astype(BF16)
            kt = (kk[:nk] * jnp.exp(bref - bcum[:nk])).astype(BF16)
            p = lax.dot_general(qt, kt, NT_DIMS, preferred_element_type=F32)
            pd = jnp.where(tri, p[:, r0:nk], 0.0)
            if blk:
                p = jnp.concatenate([p[:, :r0], pd], axis=1)
            else:
                p = pd
            outs.append(jnp.dot(p.astype(BF16), vb[:nk], preferred_element_type=F32))
        o = jnp.concatenate(outs, axis=0) + o_inter

        khat = (kk * jnp.exp(bend - bcum)).astype(BF16)
        st_ref[h] = st * jnp.exp(bend) + lax.dot_general(vb, khat, TN_DIMS, preferred_element_type=F32)

        o = o * lax.rsqrt(jnp.mean(o * o, axis=-1, keepdims=True) + EPS)
        og = g_ref[:, sl].astype(F32)
        o_ref[:, sl] = (o * ng_ref[:, sl] * (og * _sigmoid(og))).astype(o_ref.dtype)


def _hgrn(proj, batch, seq, n_pad, col0, lb_logits, norm_g):
    n = batch * seq
    width = norm_g.shape[1]
    tiles = seq // TT
    meta_blk = n // TT

    def in_row(b, t):
        return jnp.where(t == 0, meta_blk, b * tiles + t - 1)

    col = lambda c: pl.BlockSpec((TT, width), lambda b, t: (in_row(b, t), col0 + c))
    return pl.pallas_call(
        functools.partial(_hgrn_kernel, n_pad),
        grid=(batch, tiles + 1),
        in_specs=[
            col(0), col(1), col(2), col(3),
            pl.BlockSpec(lb_logits.shape, lambda b, t: (0, 0)),
            pl.BlockSpec((1, width), lambda b, t: (0, 0)),
        ],
        out_specs=pl.BlockSpec((TT, width), lambda b, t: (b * tiles + jnp.maximum(t - 1, 0), 0)),
        out_shape=jax.ShapeDtypeStruct((n, width), BF16),
        scratch_shapes=[pltpu.VMEM((width // HEAD, HEAD, HEAD), F32)],
        compiler_params=_params(("parallel", "arbitrary")),
        name="hgrn2",
    )(proj, proj, proj, proj, lb_logits, norm_g)


def _merge_kernel(ya_ref, yb_ref, wpa_ref, wpb_ref, za_ref, zb_ref, o_ref):
    ta = jnp.dot(ya_ref[...], wpa_ref[...], preferred_element_type=F32)
    tb = jnp.dot(yb_ref[...], wpb_ref[...], preferred_element_type=F32)
    mixed = _sigmoid(za_ref[...].astype(F32)) * ta + _sigmoid(zb_ref[...].astype(F32)) * tb
    o_ref[...] = mixed.astype(o_ref.dtype)


def _merge(ya, yb, wpa, wpb, proj, zcol0, tm, tn):
    n, d = ya.shape
    dm = wpa.shape[1]
    nj = dm // tn
    return pl.pallas_call(
        _merge_kernel,
        grid=(n // tm, nj),
        in_specs=[
            pl.BlockSpec((tm, d), lambda i, j: (i, 0)),
            pl.BlockSpec((tm, d), lambda i, j: (i, 0)),
            pl.BlockSpec((d, tn), lambda i, j: (0, j)),
            pl.BlockSpec((d, tn), lambda i, j: (0, j)),
            pl.BlockSpec((tm, tn), lambda i, j: (i, zcol0 + j)),
            pl.BlockSpec((tm, tn), lambda i, j: (i, zcol0 + nj + j)),
        ],
        out_specs=pl.BlockSpec((tm, tn), lambda i, j: (i, j)),
        out_shape=jax.ShapeDtypeStruct((n, dm), BF16),
        compiler_params=_params(("parallel", "arbitrary")),
        name="merge",
    )(ya, yb, wpa, wpb, proj, proj)


def _outproj_kernel(m_ref, w_ref, x_ref, g_ref, h_ref, xn_ref):
    h = x_ref[...] + jnp.dot(m_ref[...], w_ref[...], preferred_element_type=F32)
    h_ref[...] = h
    ms = jnp.mean(h * h, axis=-1, keepdims=True)
    xn_ref[...] = (h * lax.rsqrt(ms + EPS) * g_ref[...]).astype(xn_ref.dtype)


def _outproj(mixed, w_out, x, g, tm):
    n, d = x.shape
    return pl.pallas_call(
        _outproj_kernel,
        grid=(n // tm,),
        in_specs=[
            pl.BlockSpec((tm, d), lambda i: (i, 0)),
            pl.BlockSpec((d, d), lambda i: (0, 0)),
            pl.BlockSpec((tm, d), lambda i: (i, 0)),
            pl.BlockSpec((1, d), lambda i: (0, 0)),
        ],
        out_specs=[pl.BlockSpec((tm, d), lambda i: (i, 0)), pl.BlockSpec((tm, d), lambda i: (i, 0))],
        out_shape=[jax.ShapeDtypeStruct((n, d), F32), jax.ShapeDtypeStruct((n, d), BF16)],
        compiler_params=_params(("parallel",)),
        name="outproj",
    )(mixed, w_out, x, g)


def _topk_rows(s, k):
    vals = []
    for _ in range(k):
        m = jnp.max(s, axis=0, keepdims=True)
        vals.append(m)
        s = jnp.where(s == m, -jnp.inf, s)
    return vals


def _retrieve_kernel(xn_ref, wqt_ref, keys_ref, s1_ref, s2_ref, e2_ref, w1_ref, tau_ref):
    heads = keys_ref.shape[0]
    dq = keys_ref.shape[3]
    qt = lax.dot_general(wqt_ref[...], xn_ref[...], NT_DIMS, preferred_element_type=F32)
    qt = qt.astype(BF16)
    taus = []
    for h in range(heads):
        s1 = jnp.dot(keys_ref[h, 0], qt[(2 * h) * dq:(2 * h + 1) * dq], preferred_element_type=F32)
        s2 = jnp.dot(keys_ref[h, 1], qt[(2 * h + 1) * dq:(2 * h + 2) * dq], preferred_element_type=F32)
        t1 = _topk_rows(s1, PEER_TOPK)
        t2 = _topk_rows(s2, PEER_TOPK)
        t2s = jnp.concatenate(t2, axis=0)
        cands = [t1[i] + t2s for i in range(PEER_TOPK)]
        work = cands
        tau = None
        for _ in range(PEER_TOPK):
            m = functools.reduce(jnp.maximum, work)
            tau = jnp.max(m, axis=0, keepdims=True)
            work = [jnp.where(c == tau, -jnp.inf, c) for c in work]
        m1, m2 = t1[0], t2[0]
        z = jnp.zeros_like(tau)
        for i in range(PEER_TOPK):
            ex = jnp.exp(t1[i] - m1) * jnp.exp(t2s - m2)
            z = z + jnp.sum(jnp.where(cands[i] >= tau, ex, 0.0), axis=0, keepdims=True)
        s1_ref[h] = s1
        s2_ref[h] = s2
        e2_ref[h] = jnp.exp(s2 - m2)
        w1_ref[h] = jnp.exp(s1 - m1) / z
        taus.append(tau)
    tau_ref[...] = jnp.concatenate(taus, axis=0)


def _retrieve(xn, wqt, keys, tm):
    n, d = xn.shape
    heads = keys.shape[0]
    big = lambda: pl.BlockSpec((heads, N_KEYS, tm), lambda i: (0, 0, i))
    big_shape = jax.ShapeDtypeStruct((heads, N_KEYS, n), F32)
    return pl.pallas_call(
        _retrieve_kernel,
        grid=(n // tm,),
        in_specs=[
            pl.BlockSpec((tm, d), lambda i: (i, 0)),
            pl.BlockSpec(wqt.shape, lambda i: (0, 0)),
            pl.BlockSpec(keys.shape, lambda i: (0, 0, 0, 0)),
        ],
        out_specs=[big(), big(), big(), big(), pl.BlockSpec((heads, tm), lambda i: (0, i))],
        out_shape=[big_shape, big_shape, big_shape, big_shape, jax.ShapeDtypeStruct((heads, n), F32)],
        compiler_params=_params(("parallel",)),
        name="peer_retrieve",
    )(xn, wqt, keys)


def _experts_kernel(xn_ref, u_ref, v_ref, s1_ref, s2_ref, e2_ref, w1_ref, tau_ref, h_ref, g_ref,
                    o_ref, acc_ref):
    e = pl.program_id(1)
    heads = s1_ref.shape[0]
    ec = u_ref.shape[0]

    @pl.when(e == 0)
    def _():
        acc_ref[...] = jnp.zeros_like(acc_ref)

    act = lax.dot_general(u_ref[...], xn_ref[...], NT_DIMS, preferred_element_type=F32)
    ws = []
    for cc in range(ec // N_KEYS):
        c = e * (ec // N_KEYS) + cc
        gate = None
        for h in range(heads):
            s1row = s1_ref[h, pl.ds(c, 1), :]
            w1row = w1_ref[h, pl.ds(c, 1), :]
            sel = (s2_ref[h] + s1row) >= tau_ref[h:h + 1, :]
            gh = jnp.where(sel, e2_ref[h] * w1row, 0.0)
            gate = gh if gate is None else gate + gh
        ws.append((_gelu(act[cc * N_KEYS:(cc + 1) * N_KEYS]) * gate).astype(BF16))
    w = jnp.concatenate(ws, axis=0)
    acc_ref[...] += lax.dot_general(w, v_ref[...], TN_DIMS, preferred_element_type=F32)

    @pl.when(e == pl.num_programs(1) - 1)
    def _():
        hh = h_ref[...] + acc_ref[...]
        ms = jnp.mean(hh * hh, axis=-1, keepdims=True)
        o_ref[...] = hh * lax.rsqrt(ms + EPS) * g_ref[...]


def _experts(xn, u, v, s1, s2, e2, w1, tau, h2, g, tm, ec):
    n, d = xn.shape
    n_exp = u.shape[0]
    heads = s1.shape[0]
    big = lambda: pl.BlockSpec((heads, N_KEYS, tm), lambda i, e: (0, 0, i))
    return pl.pallas_call(
        _experts_kernel,
        grid=(n // tm, n_exp // ec),
        in_specs=[
            pl.BlockSpec((tm, d), lambda i, e: (i, 0)),
            pl.BlockSpec((ec, d), lambda i, e: (e, 0)),
            pl.BlockSpec((ec, d), lambda i, e: (e, 0)),
            big(), big(), big(), big(),
            pl.BlockSpec((heads, tm), lambda i, e: (0, i)),
            pl.BlockSpec((tm, d), lambda i, e: (i, 0)),
            pl.BlockSpec((1, d), lambda i, e: (0, 0)),
        ],
        out_specs=pl.BlockSpec((tm, d), lambda i, e: (i, 0)),
        out_shape=jax.ShapeDtypeStruct((n, d), F32),
        scratch_shapes=[pltpu.VMEM((tm, d), F32)],
        compiler_params=_params(("parallel", "arbitrary")),
        name="peer_experts",
    )(xn, u, v, s1, s2, e2, w1, tau, h2, g)


def _tile(n, want):
    t = min(n, want)
    assert n % t == 0, (n, t)
    return t


def kernel(x, meta, ln1_g, w_in, conv_w, conv_b, rg_wa, rg_ba, rg_wx, rg_bx, rg_lambda, hg_lb_logits,
           hg_norm_g, w_pa, w_pb, w_out, ln2_g, peer_wq, peer_keys, peer_u, peer_v, final_g):
    batch, seq, d = x.shape
    n = batch * seq
    n_meta = meta.shape[0]
    depth = w_in.shape[0]
    assert depth == 1 and seq % TT == 0 and n_meta <= TT
    rg_width = conv_w.shape[2]
    hg_width = hg_norm_g.shape[1]
    assert rg_width == hg_width == d and rg_wa.shape[2] == HEAD
    n_pad = TT - n_meta
    row = lambda a: a.reshape(1, -1)

    xf = x.reshape(n, d)
    meta_tile = jnp.concatenate([jnp.zeros((n_pad, d), x.dtype), meta.astype(x.dtype)], axis=0)

    w_in_b = w_in[0].astype(BF16)
    tm = _tile(n, 1024)
    proj = _norm_matmul(xf, row(ln1_g[0]), w_in_b, n + TT, 0, tm, 1024)
    proj = _norm_matmul(meta_tile, row(ln1_g[0]), w_in_b, n + TT, n // TT, TT, 1024, prev=proj)

    y_a = _rglru(proj, batch, seq, n_pad, conv_w[0], row(conv_b[0]), rg_wa[0].astype(BF16), row(rg_ba[0]),
                 rg_wx[0].astype(BF16), row(rg_bx[0]), row(rg_lambda[0]))
    y_b = _hgrn(proj, batch, seq, n_pad, 2, hg_lb_logits, row(hg_norm_g[0]))

    tn = 1024
    mixed = _merge(y_a, y_b, w_pa[0].astype(BF16), w_pb[0].astype(BF16), proj, 6 * (d // tn), tm, tn)
    h2, xn2 = _outproj(mixed, w_out[0].astype(BF16), xf, row(ln2_g[0]), _tile(n, 512))

    tp = _tile(n, 512)
    s1, s2, e2, w1, tau = _retrieve(xn2, peer_wq[0].T.astype(BF16), peer_keys[0].astype(BF16), tp)
    out = _experts(xn2, peer_u[0].astype(BF16), peer_v[0].astype(BF16), s1, s2, e2, w1, tau, h2,
                   row(final_g), tp, 512)
    return out.reshape(batch, seq, d)
```

```python
import functools
import math

import jax
import jax.numpy as jnp
from jax import lax
from jax.experimental import pallas as pl
from jax.experimental.pallas import tpu as pltpu

F32 = jnp.float32
BF16 = jnp.bfloat16

EPS = 1e-6
RG_C = 8.0
CONV_WIDTH = 4
HEAD = 128
N_KEYS = 128
PEER_TOPK = 16
TT = 128
RB = 32
LANES = 128
VMEM_LIMIT = 56 * 1024 * 1024

NT_DIMS = (((1,), (1,)), ((), ()))
TN_DIMS = (((0,), (0,)), ((), ()))


def _params(sem):
    return pltpu.CompilerParams(dimension_semantics=sem, vmem_limit_bytes=VMEM_LIMIT)


def _sigmoid(x):
    return jax.nn.sigmoid(x)


def _gelu(x):
    return 0.5 * x * (1.0 + lax.erf(x * (1.0 / math.sqrt(2.0))))


def _norm_mm_kernel(x_ref, g_ref, w_ref, *rest):
    o_ref, xn_ref = rest[-2], rest[-1]

    @pl.when(pl.program_id(1) == 0)
    def _():
        xf = x_ref[...]
        ms = jnp.mean(xf * xf, axis=-1, keepdims=True)
        xn_ref[...] = (xf * lax.rsqrt(ms + EPS) * g_ref[...]).astype(BF16)

    o_ref[...] = jnp.dot(xn_ref[...], w_ref[...], preferred_element_type=F32).astype(o_ref.dtype)


def _norm_matmul(x, g, w, out_rows, row_block_offset, tm, tn, prev=None):
    m, d = x.shape
    n = w.shape[1]
    in_specs = [
        pl.BlockSpec((tm, d), lambda i, j: (i, 0)),
        pl.BlockSpec((1, d), lambda i, j: (0, 0)),
        pl.BlockSpec((d, tn), lambda i, j: (0, j)),
    ]
    args = [x, g, w]
    aliases = {}
    if prev is not None:
        in_specs.append(pl.BlockSpec(memory_space=pl.ANY))
        args.append(prev)
        aliases = {3: 0}
    return pl.pallas_call(
        _norm_mm_kernel,
        grid=(m // tm, n // tn),
        in_specs=in_specs,
        out_specs=pl.BlockSpec((tm, tn), lambda i, j: (i + row_block_offset, j)),
        out_shape=jax.ShapeDtypeStruct((out_rows, n), BF16),
        scratch_shapes=[pltpu.VMEM((tm, d), BF16)],
        input_output_aliases=aliases,
        compiler_params=_params(("parallel", "arbitrary")),
        name="norm_matmul",
    )(*args)


def _rglru_kernel(n_pad, xa_ref, ya_ref, cw_ref, cb_ref, wa_ref, ba_ref, wx_ref, bx_ref, lam_ref,
                  o_ref, xbuf_ref, hc_ref):
    t = pl.program_id(1)
    width = xa_ref.shape[1]

    @pl.when(t == 0)
    def _():
        xbuf_ref[0:8, :] = jnp.zeros((8, width), F32)
        hc_ref[...] = jnp.zeros_like(hc_ref)

    xbuf_ref[8:8 + TT, :] = xa_ref[...].astype(F32)
    row = lax.broadcasted_iota(jnp.int32, (TT, HEAD), 0)
    live = jnp.logical_or(t > 0, row >= n_pad)
    r8 = row & 7

    for h in range(width // HEAD):
        sl = slice(h * HEAD, (h + 1) * HEAD)
        xc = cb_ref[:, sl] + cw_ref[0:1, sl] * xbuf_ref[5:5 + TT, sl]
        for k in range(1, CONV_WIDTH):
            xc = xc + cw_ref[k:k + 1, sl] * xbuf_ref[5 + k:5 + k + TT, sl]
        xcb = xc.astype(BF16)
        r = _sigmoid(jnp.dot(xcb, wa_ref[h], preferred_element_type=F32) + ba_ref[:, sl])
        i = _sigmoid(jnp.dot(xcb, wx_ref[h], preferred_element_type=F32) + bx_ref[:, sl])
        sp = jax.nn.softplus(-lam_ref[:, sl])
        log_a = (-RG_C) * r * sp
        a = jnp.exp(log_a)
        th = jnp.tanh(log_a)
        u = jnp.sqrt(-2.0 * th / (1.0 - th)) * (i * xc)
        u = jnp.where(live, u, 0.0)
        for k in (1, 2, 4):
            a_sh = pltpu.roll(a, k, 0)
            u_sh = pltpu.roll(u, k, 0)
            m = r8 >= k
            u = jnp.where(m, a * u_sh + u, u)
            a = jnp.where(m, a * a_sh, a)
        carry = hc_ref[:, sl]
        outs = []
        for g in range(TT // 8):
            hg = a[g * 8:(g + 1) * 8] * carry + u[g * 8:(g + 1) * 8]
            outs.append(hg)
            carry = hg[7:8]
        hc_ref[:, sl] = carry
        hs = jnp.concatenate(outs, axis=0)
        o_ref[:, sl] = (_gelu(ya_ref[:, sl].astype(F32)) * hs).astype(o_ref.dtype)

    xbuf_ref[0:8, :] = xbuf_ref[TT:TT + 8, :]


def _rglru(proj, batch, seq, n_pad, conv_w, conv_b, wa, ba, wx, bx, lam):
    n = batch * seq
    width = conv_w.shape[1]
    tiles = seq // TT
    meta_blk = n // TT

    def in_row(b, t):
        return jnp.where(t == 0, meta_blk, b * tiles + t - 1)

    vec = lambda: pl.BlockSpec((1, width), lambda b, t: (0, 0))
    gate_w = lambda: pl.BlockSpec((width // HEAD, HEAD, HEAD), lambda b, t: (0, 0, 0))
    return pl.pallas_call(
        functools.partial(_rglru_kernel, n_pad),
        grid=(batch, tiles + 1),
        in_specs=[
            pl.BlockSpec((TT, width), lambda b, t: (in_row(b, t), 0)),
            pl.BlockSpec((TT, width), lambda b, t: (in_row(b, t), 1)),
            pl.BlockSpec((CONV_WIDTH, width), lambda b, t: (0, 0)),
            vec(), gate_w(), vec(), gate_w(), vec(), vec(),
        ],
        out_specs=pl.BlockSpec((TT, width), lambda b, t: (b * tiles + jnp.maximum(t - 1, 0), 0)),
        out_shape=jax.ShapeDtypeStruct((n, width), BF16),
        scratch_shapes=[pltpu.VMEM((TT + 8, width), F32), pltpu.VMEM((1, width), F32)],
        compiler_params=_params(("parallel", "arbitrary")),
        name="rglru",
    )(proj, proj, conv_w, conv_b, wa, ba, wx, bx, lam)


def _hgrn_kernel(n_pad, q_ref, f_ref, v_ref, g_ref, lbl_ref, ng_ref, o_ref, st_ref):
    t = pl.program_id(1)
    width = q_ref.shape[1]

    @pl.when(t == 0)
    def _():
        st_ref[...] = jnp.zeros_like(st_ref)

    row = lax.broadcasted_iota(jnp.int32, (TT, HEAD), 0)
    live = jnp.logical_or(t > 0, row >= n_pad)
    r8 = row & 7
    tri = (lax.broadcasted_iota(jnp.int32, (RB, RB), 0) >= lax.broadcasted_iota(jnp.int32, (RB, RB), 1))

    for h in range(width // HEAD):
        sl = slice(h * HEAD, (h + 1) * HEAD)
        lg = lbl_ref[:, sl]
        e = jnp.exp(lg - jnp.max(lg, axis=0, keepdims=True))
        lb = e[0:1] / jnp.sum(e, axis=0, keepdims=True)
        f = lb + (1.0 - lb) * _sigmoid(f_ref[:, sl].astype(F32))
        lf = jnp.where(live, jnp.log(f), 0.0)
        kk = 1.0 - f
        q = q_ref[:, sl].astype(F32)
        qs = q * _sigmoid(q)
        vb = v_ref[:, sl]

        c = lf
        for k in (1, 2, 4):
            c = c + jnp.where(r8 >= k, pltpu.roll(c, k, 0), 0.0)
        parts = []
        off = jnp.zeros((1, HEAD), F32)
        for g in range(TT // 8):
            cg = c[g * 8:(g + 1) * 8] + off
            parts.append(cg)
            off = cg[7:8]
        bcum = jnp.concatenate(parts, axis=0)
        bend = off

        st = st_ref[h]
        qhat = (qs * jnp.exp(bcum)).astype(BF16)
        o_inter = lax.dot_general(qhat, st.astype(BF16), NT_DIMS, preferred_element_type=F32)

        outs = []
        for blk in range(TT // RB):
            r0 = blk * RB
            nk = r0 + RB
            mid = r0 + RB // 2 - 1
            bref = bcum[mid:mid + 1]
            qt = (qs[r0:nk] * jnp.exp(bcum[r0:nk] - bref)).astype(BF16)
            kt = (kk[:nk] * jnp.exp(bref - bcum[:nk])).astype(BF16)
            p = lax.dot_general(qt, kt, NT_DIMS, preferred_element_type=F32)
            pd = jnp.where(tri, p[:, r0:nk], 0.0)
            if blk:
                p = jnp.concatenate([p[:, :r0], pd], axis=1)
            else:
                p = pd
            outs.append(jnp.dot(p.astype(BF16), vb[:nk], preferred_element_type=F32))
        o = jnp.concatenate(outs, axis=0) + o_inter

        khat = (kk * jnp.exp(bend - bcum)).astype(BF16)
        st_ref[h] = st * jnp.exp(bend) + lax.dot_general(vb, khat, TN_DIMS, preferred_element_type=F32)

        o = o * lax.rsqrt(jnp.mean(o * o, axis=-1, keepdims=True) + EPS)
        og = g_ref[:, sl].astype(F32)
        o_ref[:, sl] = (o * ng_ref[:, sl] * (og * _sigmoid(og))).astype(o_ref.dtype)


def _hgrn(proj, batch, seq, n_pad, col0, lb_logits, norm_g):
    n = batch * seq
    width = norm_g.shape[1]
    tiles = seq // TT
    meta_blk = n // TT

    def in_row(b, t):
        return jnp.where(t == 0, meta_blk, b * tiles + t - 1)

    col = lambda c: pl.BlockSpec((TT, width), lambda b, t: (in_row(b, t), col0 + c))
    return pl.pallas_call(
        functools.partial(_hgrn_kernel, n_pad),
        grid=(batch, tiles + 1),
        in_specs=[
            col(0), col(1), col(2), col(3),
            pl.BlockSpec(lb_logits.shape, lambda b, t: (0, 0)),
            pl.BlockSpec((1, width), lambda b, t: (0, 0)),
        ],
        out_specs=pl.BlockSpec((TT, width), lambda b, t: (b * tiles + jnp.maximum(t - 1, 0), 0)),
        out_shape=jax.ShapeDtypeStruct((n, width), BF16),
        scratch_shapes=[pltpu.VMEM((width // HEAD, HEAD, HEAD), F32)],
        compiler_params=_params(("parallel", "arbitrary")),
        name="hgrn2",
    )(proj, proj, proj, proj, lb_logits, norm_g)


def _merge_kernel(ya_ref, yb_ref, wpa_ref, wpb_ref, za_ref, zb_ref, o_ref):
    ta = jnp.dot(ya_ref[...], wpa_ref[...], preferred_element_type=F32)
    tb = jnp.dot(yb_ref[...], wpb_ref[...], preferred_element_type=F32)
    mixed = _sigmoid(za_ref[...].astype(F32)) * ta + _sigmoid(zb_ref[...].astype(F32)) * tb
    o_ref[...] = mixed.astype(o_ref.dtype)


def _merge(ya, yb, wpa, wpb, proj, zcol0, tm, tn):
    n, d = ya.shape
    dm = wpa.shape[1]
    nj = dm // tn
    return pl.pallas_call(
        _merge_kernel,
        grid=(n // tm, nj),
        in_specs=[
            pl.BlockSpec((tm, d), lambda i, j: (i, 0)),
            pl.BlockSpec((tm, d), lambda i, j: (i, 0)),
            pl.BlockSpec((d, tn), lambda i, j: (0, j)),
            pl.BlockSpec((d, tn), lambda i, j: (0, j)),
            pl.BlockSpec((tm, tn), lambda i, j: (i, zcol0 + j)),
            pl.BlockSpec((tm, tn), lambda i, j: (i, zcol0 + nj + j)),
        ],
        out_specs=pl.BlockSpec((tm, tn), lambda i, j: (i, j)),
        out_shape=jax.ShapeDtypeStruct((n, dm), BF16),
        compiler_params=_params(("parallel", "arbitrary")),
        name="merge",
    )(ya, yb, wpa, wpb, proj, proj)


def _outproj_kernel(m_ref, w_ref, x_ref, g_ref, h_ref, xnt_ref):
    h = x_ref[...] + jnp.dot(m_ref[...], w_ref[...], preferred_element_type=F32)
    h_ref[...] = h
    ms = jnp.mean(h * h, axis=-1, keepdims=True)
    xn = h * lax.rsqrt(ms + EPS) * g_ref[...]
    xnt_ref[...] = xn.T.astype(xnt_ref.dtype)


def _outproj(mixed, w_out, x, g, tm):
    n, d = x.shape
    return pl.pallas_call(
        _outproj_kernel,
        grid=(n // tm,),
        in_specs=[
            pl.BlockSpec((tm, d), lambda i: (i, 0)),
            pl.BlockSpec((d, d), lambda i: (0, 0)),
            pl.BlockSpec((tm, d), lambda i: (i, 0)),
            pl.BlockSpec((1, d), lambda i: (0, 0)),
        ],
        out_specs=[pl.BlockSpec((tm, d), lambda i: (i, 0)), pl.BlockSpec((d, tm), lambda i: (0, i))],
        out_shape=[jax.ShapeDtypeStruct((n, d), F32), jax.ShapeDtypeStruct((d, n), BF16)],
        compiler_params=_params(("parallel",)),
        name="outproj",
    )(mixed, w_out, x, g)


def _top_ranks(s, k):
    rank = jnp.full(s.shape, float(N_KEYS), F32)
    vals = []
    for j in range(k):
        m = jnp.max(s, axis=0, keepdims=True)
        hit = s == m
        rank = jnp.where(hit, float(j), rank)
        s = jnp.where(hit, -jnp.inf, s)
        vals.append(m)
    return rank, vals


def _retrieve_kernel(xnt_ref, wqt_ref, keys_ref, r2_ref, e2_ref, m_ref, w_ref):
    heads = keys_ref.shape[0]
    dq = keys_ref.shape[3]
    k = PEER_TOPK
    tm = xnt_ref.shape[1]
    qt = jnp.dot(wqt_ref[...], xnt_ref[...], preferred_element_type=F32).astype(BF16)
    for h in range(heads):
        s1_all = jnp.dot(keys_ref[h, 0], qt[(2 * h) * dq:(2 * h + 1) * dq], preferred_element_type=F32)
        s2_all = jnp.dot(keys_ref[h, 1], qt[(2 * h + 1) * dq:(2 * h + 2) * dq], preferred_element_type=F32)
        for lo in range(0, tm, LANES):
            cols = slice(lo, lo + LANES)
            s1, s2 = s1_all[:, cols], s2_all[:, cols]
            r1, t1 = _top_ranks(s1, k)
            r2, t2 = _top_ranks(s2, k)
            t2s = jnp.concatenate(t2, axis=0)
            pieces = [t1[i] + t2s[:k // (i + 1)] for i in range(k)]
            rows = sum(p.shape[0] for p in pieces)
            pieces.append(jnp.full((-rows % 8, LANES), -jnp.inf, F32))
            work = jnp.concatenate(pieces, axis=0)
            tau = None
            for _ in range(k):
                tau = jnp.max(work, axis=0, keepdims=True)
                work = jnp.where(work == tau, -jnp.inf, work)
            e2s = jnp.exp(t2s - t2[0])
            z = jnp.zeros_like(tau)
            m = jnp.zeros_like(s1)
            for i in range(k):
                sel = (t1[i] + t2s) >= tau
                cnt = jnp.sum(sel.astype(F32), axis=0, keepdims=True)
                z = z + jnp.exp(t1[i] - t1[0]) * jnp.sum(jnp.where(sel, e2s, 0.0), axis=0, keepdims=True)
                m = jnp.where(r1 == float(i), cnt, m)
            r2_ref[h, :, cols] = r2.astype(r2_ref.dtype)
            e2_ref[h, :, cols] = jnp.exp(s2 - t2[0]).astype(e2_ref.dtype)
            m_ref[h, :, cols] = m
            w_ref[h, :, cols] = (0.5 * jnp.exp(s1 - t1[0])) / z


def _retrieve(xnt, wqt, keys, tm):
    d, n = xnt.shape
    heads = keys.shape[0]
    assert tm % LANES == 0
    big = lambda: pl.BlockSpec((heads, N_KEYS, tm), lambda i: (0, 0, i))
    shape = lambda dt: jax.ShapeDtypeStruct((heads, N_KEYS, n), dt)
    return pl.pallas_call(
        _retrieve_kernel,
        grid=(n // tm,),
        in_specs=[
            pl.BlockSpec((d, tm), lambda i: (0, i)),
            pl.BlockSpec(wqt.shape, lambda i: (0, 0)),
            pl.BlockSpec(keys.shape, lambda i: (0, 0, 0, 0)),
        ],
        out_specs=[big(), big(), big(), big()],
        out_shape=[shape(BF16), shape(BF16), shape(F32), shape(F32)],
        compiler_params=_params(("parallel",)),
        name="peer_retrieve",
    )(xnt, wqt, keys)


def _experts_kernel(n_blk, xnt_ref, u_ref, v_ref, r2_ref, e2_ref, m_ref, w_ref, h_ref, g_ref, o_ref,
                    act0_ref, act1_ref, acc_ref):
    s = pl.program_id(0)
    heads = r2_ref.shape[0]
    ec = u_ref.shape[0]
    e_prev = jnp.maximum(s - 1, 0) % n_blk

    @pl.when(s == 0)
    def _():
        act1_ref[...] = jnp.zeros_like(act1_ref)

    @pl.when(e_prev == 0)
    def _():
        acc_ref[...] = jnp.zeros_like(acc_ref)

    def step(src_ref, dst_ref):
        dst_ref[...] = jnp.dot(u_ref[...], xnt_ref[...], preferred_element_type=F32)
        ws = []
        for cc in range(ec // N_KEYS):
            rows = slice(cc * N_KEYS, (cc + 1) * N_KEYS)
            c = e_prev * (ec // N_KEYS) + cc
            gate = None
            for h in range(heads):
                mrow = m_ref[h, pl.ds(c, 1), :].astype(BF16)
                wrow = w_ref[h, pl.ds(c, 1), :].astype(BF16)
                gh = jnp.where(r2_ref[h] < mrow, e2_ref[h], 0) * wrow
                gate = gh if gate is None else gate + gh
            a = src_ref[rows, :]
            ge = a * (1.0 + lax.erf(a * (1.0 / math.sqrt(2.0))))
            ws.append(ge.astype(BF16) * gate)
        w = jnp.concatenate(ws, axis=0)
        acc_ref[...] += lax.dot_general(w, v_ref[...], TN_DIMS, preferred_element_type=F32)

    @pl.when(s % 2 == 0)
    def _():
        step(act1_ref, act0_ref)

    @pl.when(s % 2 == 1)
    def _():
        step(act0_ref, act1_ref)

    @pl.when(jnp.logical_and(s > 0, e_prev == n_blk - 1))
    def _():
        hh = h_ref[...] + acc_ref[...]
        ms = jnp.mean(hh * hh, axis=-1, keepdims=True)
        o_ref[...] = hh * lax.rsqrt(ms + EPS) * g_ref[...]


def _experts(xnt, u, v, r2, e2, m, w, h2, g, tm, ec):
    d, n = xnt.shape
    n_exp = u.shape[0]
    heads = r2.shape[0]
    n_blk = n_exp // ec
    assert ec % N_KEYS == 0
    steps = (n // tm) * n_blk
    cur = lambda s: jnp.minimum(s, steps - 1)
    prev = lambda s: jnp.maximum(s - 1, 0)
    tile_spec = lambda shape, imap: pl.BlockSpec(shape, imap, pipeline_mode=pl.Buffered(1))
    big = lambda: tile_spec((heads, N_KEYS, tm), lambda s: (0, 0, prev(s) // n_blk))
    return pl.pallas_call(
        functools.partial(_experts_kernel, n_blk),
        grid=(steps + 1,),
        in_specs=[
            pl.BlockSpec((d, tm), lambda s: (0, cur(s) // n_blk)),
            pl.BlockSpec((ec, d), lambda s: (cur(s) % n_blk, 0)),
            pl.BlockSpec((ec, d), lambda s: (prev(s) % n_blk, 0)),
            big(), big(), big(), big(),
            tile_spec((tm, d), lambda s: (prev(s) // n_blk, 0)),
            pl.BlockSpec((1, d), lambda s: (0, 0)),
        ],
        out_specs=pl.BlockSpec((tm, d), lambda s: (prev(s) // n_blk, 0)),
        out_shape=jax.ShapeDtypeStruct((n, d), F32),
        scratch_shapes=[pltpu.VMEM((ec, tm), F32), pltpu.VMEM((ec, tm), F32), pltpu.VMEM((tm, d), F32)],
        compiler_params=_params(("arbitrary",)),
        name="peer_experts",
    )(xnt, u, v, r2, e2, m, w, h2, g)


def _tile(n, want):
    t = min(n, want)
    assert n % t == 0, (n, t)
    return t


def kernel(x, meta, ln1_g, w_in, conv_w, conv_b, rg_wa, rg_ba, rg_wx, rg_bx, rg_lambda, hg_lb_logits,
           hg_norm_g, w_pa, w_pb, w_out, ln2_g, peer_wq, peer_keys, peer_u, peer_v, final_g):
    batch, seq, d = x.shape
    n = batch * seq
    n_meta = meta.shape[0]
    depth = w_in.shape[0]
    assert depth == 1 and seq % TT == 0 and n_meta <= TT
    rg_width = conv_w.shape[2]
    hg_width = hg_norm_g.shape[1]
    assert rg_width == hg_width == d and rg_wa.shape[2] == HEAD
    n_pad = TT - n_meta
    row = lambda a: a.reshape(1, -1)

    xf = x.reshape(n, d)
    meta_tile = jnp.concatenate([jnp.zeros((n_pad, d), x.dtype), meta.astype(x.dtype)], axis=0)

    w_in_b = w_in[0].astype(BF16)
    tm = _tile(n, 1024)
    proj = _norm_matmul(xf, row(ln1_g[0]), w_in_b, n + TT, 0, tm, 1024)
    proj = _norm_matmul(meta_tile, row(ln1_g[0]), w_in_b, n + TT, n // TT, TT, 1024, prev=proj)

    y_a = _rglru(proj, batch, seq, n_pad, conv_w[0], row(conv_b[0]), rg_wa[0].astype(BF16), row(rg_ba[0]),
                 rg_wx[0].astype(BF16), row(rg_bx[0]), row(rg_lambda[0]))
    y_b = _hgrn(proj, batch, seq, n_pad, 2, hg_lb_logits, row(hg_norm_g[0]))

    tn = 1024
    mixed = _merge(y_a, y_b, w_pa[0].astype(BF16), w_pb[0].astype(BF16), proj, 6 * (d // tn), tm, tn)
    h2, xn2t = _outproj(mixed, w_out[0].astype(BF16), xf, row(ln2_g[0]), _tile(n, 512))

    tp = _tile(n, 512)
    r2, e2, m, w = _retrieve(xn2t, peer_wq[0].T.astype(BF16), peer_keys[0].astype(BF16), tp)
    out = _experts(xn2t, peer_u[0].astype(BF16), peer_v[0].astype(BF16), r2, e2, m, w, h2,
                   row(final_g), tp, 1024)
    return out.reshape(batch, seq, d)
```

```python
import functools
import math

import jax
import jax.numpy as jnp
from jax import lax
from jax.experimental import pallas as pl
from jax.experimental.pallas import tpu as pltpu

F32 = jnp.float32
BF16 = jnp.bfloat16

EPS = 1e-6
RG_C = 8.0
CONV_WIDTH = 4
HEAD = 128
N_KEYS = 128
PEER_TOPK = 16
TT = 128
RB = 32
LANES = 128
WEIGHT_SPLIT = 4
VMEM_LIMIT = 56 * 1024 * 1024

NT_DIMS = (((1,), (1,)), ((), ()))
TN_DIMS = (((0,), (0,)), ((), ()))


def _params(sem):
    return pltpu.CompilerParams(dimension_semantics=sem, vmem_limit_bytes=VMEM_LIMIT)


def _sigmoid(x):
    return jax.nn.sigmoid(x)


def _gelu(x):
    return 0.5 * x * (1.0 + lax.erf(x * (1.0 / math.sqrt(2.0))))


def _norm_mm_kernel(x_ref, g_ref, w_ref, *rest):
    o_ref, xn_ref = rest[-2], rest[-1]

    @pl.when(pl.program_id(1) == 0)
    def _():
        xf = x_ref[...]
        ms = jnp.mean(xf * xf, axis=-1, keepdims=True)
        xn_ref[...] = (xf * lax.rsqrt(ms + EPS) * g_ref[...]).astype(BF16)

    o_ref[...] = jnp.dot(xn_ref[...], w_ref[...], preferred_element_type=F32).astype(o_ref.dtype)


def _norm_matmul(x, g, w, out_rows, row_block_offset, tm, tn, prev=None):
    m, d = x.shape
    n = w.shape[1]
    in_specs = [
        pl.BlockSpec((tm, d), lambda i, j: (i, 0)),
        pl.BlockSpec((1, d), lambda i, j: (0, 0)),
        pl.BlockSpec((d, tn), lambda i, j: (0, j)),
    ]
    args = [x, g, w]
    aliases = {}
    if prev is not None:
        in_specs.append(pl.BlockSpec(memory_space=pl.ANY))
        args.append(prev)
        aliases = {3: 0}
    return pl.pallas_call(
        _norm_mm_kernel,
        grid=(m // tm, n // tn),
        in_specs=in_specs,
        out_specs=pl.BlockSpec((tm, tn), lambda i, j: (i + row_block_offset, j)),
        out_shape=jax.ShapeDtypeStruct((out_rows, n), BF16),
        scratch_shapes=[pltpu.VMEM((tm, d), BF16)],
        input_output_aliases=aliases,
        compiler_params=_params(("parallel", "arbitrary")),
        name="norm_matmul",
    )(*args)


def _rglru_kernel(n_pad, xa_ref, ya_ref, cw_ref, cb_ref, wa_ref, ba_ref, wx_ref, bx_ref, lam_ref,
                  o_ref, xbuf_ref, hc_ref):
    t = pl.program_id(1)
    width = xa_ref.shape[1]

    @pl.when(t == 0)
    def _():
        xbuf_ref[0:8, :] = jnp.zeros((8, width), F32)
        hc_ref[...] = jnp.zeros_like(hc_ref)

    xbuf_ref[8:8 + TT, :] = xa_ref[...].astype(F32)
    row = lax.broadcasted_iota(jnp.int32, (TT, HEAD), 0)
    live = jnp.logical_or(t > 0, row >= n_pad)
    r8 = row & 7

    for h in range(width // HEAD):
        sl = slice(h * HEAD, (h + 1) * HEAD)
        xc = cb_ref[:, sl] + cw_ref[0:1, sl] * xbuf_ref[5:5 + TT, sl]
        for k in range(1, CONV_WIDTH):
            xc = xc + cw_ref[k:k + 1, sl] * xbuf_ref[5 + k:5 + k + TT, sl]
        xcb = xc.astype(BF16)
        r = _sigmoid(jnp.dot(xcb, wa_ref[h], preferred_element_type=F32) + ba_ref[:, sl])
        i = _sigmoid(jnp.dot(xcb, wx_ref[h], preferred_element_type=F32) + bx_ref[:, sl])
        sp = jax.nn.softplus(-lam_ref[:, sl])
        log_a = (-RG_C) * r * sp
        a = jnp.exp(log_a)
        th = jnp.tanh(log_a)
        u = jnp.sqrt(-2.0 * th / (1.0 - th)) * (i * xc)
        u = jnp.where(live, u, 0.0)
        for k in (1, 2, 4):
            a_sh = pltpu.roll(a, k, 0)
            u_sh = pltpu.roll(u, k, 0)
            m = r8 >= k
            u = jnp.where(m, a * u_sh + u, u)
            a = jnp.where(m, a * a_sh, a)
        carry = hc_ref[:, sl]
        outs = []
        for g in range(TT // 8):
            hg = a[g * 8:(g + 1) * 8] * carry + u[g * 8:(g + 1) * 8]
            outs.append(hg)
            carry = hg[7:8]
        hc_ref[:, sl] = carry
        hs = jnp.concatenate(outs, axis=0)
        o_ref[:, sl] = (_gelu(ya_ref[:, sl].astype(F32)) * hs).astype(o_ref.dtype)

    xbuf_ref[0:8, :] = xbuf_ref[TT:TT + 8, :]


def _rglru(proj, batch, seq, n_pad, conv_w, conv_b, wa, ba, wx, bx, lam):
    n = batch * seq
    width = conv_w.shape[1]
    tiles = seq // TT
    meta_blk = n // TT

    def in_row(b, t):
        return jnp.where(t == 0, meta_blk, b * tiles + t - 1)

    vec = lambda: pl.BlockSpec((1, width), lambda b, t: (0, 0))
    gate_w = lambda: pl.BlockSpec((width // HEAD, HEAD, HEAD), lambda b, t: (0, 0, 0))
    return pl.pallas_call(
        functools.partial(_rglru_kernel, n_pad),
        grid=(batch, tiles + 1),
        in_specs=[
            pl.BlockSpec((TT, width), lambda b, t: (in_row(b, t), 0)),
            pl.BlockSpec((TT, width), lambda b, t: (in_row(b, t), 1)),
            pl.BlockSpec((CONV_WIDTH, width), lambda b, t: (0, 0)),
            vec(), gate_w(), vec(), gate_w(), vec(), vec(),
        ],
        out_specs=pl.BlockSpec((TT, width), lambda b, t: (b * tiles + jnp.maximum(t - 1, 0), 0)),
        out_shape=jax.ShapeDtypeStruct((n, width), BF16),
        scratch_shapes=[pltpu.VMEM((TT + 8, width), F32), pltpu.VMEM((1, width), F32)],
        compiler_params=_params(("parallel", "arbitrary")),
        name="rglru",
    )(proj, proj, conv_w, conv_b, wa, ba, wx, bx, lam)


def _hgrn_kernel(n_pad, q_ref, f_ref, v_ref, g_ref, lbl_ref, ng_ref, o_ref, st_ref):
    t = pl.program_id(1)
    width = q_ref.shape[1]

    @pl.when(t == 0)
    def _():
        st_ref[...] = jnp.zeros_like(st_ref)

    row = lax.broadcasted_iota(jnp.int32, (TT, HEAD), 0)
    live = jnp.logical_or(t > 0, row >= n_pad)
    r8 = row & 7
    tri = (lax.broadcasted_iota(jnp.int32, (RB, RB), 0) >= lax.broadcasted_iota(jnp.int32, (RB, RB), 1))

    for h in range(width // HEAD):
        sl = slice(h * HEAD, (h + 1) * HEAD)
        lg = lbl_ref[:, sl]
        e = jnp.exp(lg - jnp.max(lg, axis=0, keepdims=True))
        lb = e[0:1] / jnp.sum(e, axis=0, keepdims=True)
        f = lb + (1.0 - lb) * _sigmoid(f_ref[:, sl].astype(F32))
        lf = jnp.where(live, jnp.log(f), 0.0)
        kk = 1.0 - f
        q = q_ref[:, sl].astype(F32)
        qs = q * _sigmoid(q)
        vb = v_ref[:, sl]

        c = lf
        for k in (1, 2, 4):
            c = c + jnp.where(r8 >= k, pltpu.roll(c, k, 0), 0.0)
        parts = []
        off = jnp.zeros((1, HEAD), F32)
        for g in range(TT // 8):
            cg = c[g * 8:(g + 1) * 8] + off
            parts.append(cg)
            off = cg[7:8]
        bcum = jnp.concatenate(parts, axis=0)
        bend = off

        st = st_ref[h]
        qhat = (qs * jnp.exp(bcum)).astype(BF16)
        o_inter = lax.dot_general(qhat, st.astype(BF16), NT_DIMS, preferred_element_type=F32)

        outs = []
        for blk in range(TT // RB):
            r0 = blk * RB
            nk = r0 + RB
            mid = r0 + RB // 2 - 1
            bref = bcum[mid:mid + 1]
            qt = (qs[r0:nk] * jnp.exp(bcum[r0:nk] - bref)).astype(BF16)
            kt = (kk[:nk] * jnp.exp(bref - bcum[:nk])).astype(BF16)
            p = lax.dot_general(qt, kt, NT_DIMS, preferred_element_type=F32)
            pd = jnp.where(tri, p[:, r0:nk], 0.0)
            if blk:
                p = jnp.concatenate([p[:, :r0], pd], axis=1)
            else:
                p = pd
            outs.append(jnp.dot(p.astype(BF16), vb[:nk], preferred_element_type=F32))
        o = jnp.concatenate(outs, axis=0) + o_inter

        khat = (kk * jnp.exp(bend - bcum)).astype(BF16)
        st_ref[h] = st * jnp.exp(bend) + lax.dot_general(vb, khat, TN_DIMS, preferred_element_type=F32)

        o = o * lax.rsqrt(jnp.mean(o * o, axis=-1, keepdims=True) + EPS)
        og = g_ref[:, sl].astype(F32)
        o_ref[:, sl] = (o * ng_ref[:, sl] * (og * _sigmoid(og))).astype(o_ref.dtype)


def _hgrn(proj, batch, seq, n_pad, col0, lb_logits, norm_g):
    n = batch * seq
    width = norm_g.shape[1]
    tiles = seq // TT
    meta_blk = n // TT

    def in_row(b, t):
        return jnp.where(t == 0, meta_blk, b * tiles + t - 1)

    col = lambda c: pl.BlockSpec((TT, width), lambda b, t: (in_row(b, t), col0 + c))
    return pl.pallas_call(
        functools.partial(_hgrn_kernel, n_pad),
        grid=(batch, tiles + 1),
        in_specs=[
            col(0), col(1), col(2), col(3),
            pl.BlockSpec(lb_logits.shape, lambda b, t: (0, 0)),
            pl.BlockSpec((1, width), lambda b, t: (0, 0)),
        ],
        out_specs=pl.BlockSpec((TT, width), lambda b, t: (b * tiles + jnp.maximum(t - 1, 0), 0)),
        out_shape=jax.ShapeDtypeStruct((n, width), BF16),
        scratch_shapes=[pltpu.VMEM((width // HEAD, HEAD, HEAD), F32)],
        compiler_params=_params(("parallel", "arbitrary")),
        name="hgrn2",
    )(proj, proj, proj, proj, lb_logits, norm_g)


def _merge_kernel(ya_ref, yb_ref, wpa_ref, wpb_ref, za_ref, zb_ref, o_ref):
    ta = jnp.dot(ya_ref[...], wpa_ref[...], preferred_element_type=F32)
    tb = jnp.dot(yb_ref[...], wpb_ref[...], preferred_element_type=F32)
    mixed = _sigmoid(za_ref[...].astype(F32)) * ta + _sigmoid(zb_ref[...].astype(F32)) * tb
    o_ref[...] = mixed.astype(o_ref.dtype)


def _merge(ya, yb, wpa, wpb, proj, zcol0, tm, tn):
    n, d = ya.shape
    dm = wpa.shape[1]
    nj = dm // tn
    return pl.pallas_call(
        _merge_kernel,
        grid=(n // tm, nj),
        in_specs=[
            pl.BlockSpec((tm, d), lambda i, j: (i, 0)),
            pl.BlockSpec((tm, d), lambda i, j: (i, 0)),
            pl.BlockSpec((d, tn), lambda i, j: (0, j)),
            pl.BlockSpec((d, tn), lambda i, j: (0, j)),
            pl.BlockSpec((tm, tn), lambda i, j: (i, zcol0 + j)),
            pl.BlockSpec((tm, tn), lambda i, j: (i, zcol0 + nj + j)),
        ],
        out_specs=pl.BlockSpec((tm, tn), lambda i, j: (i, j)),
        out_shape=jax.ShapeDtypeStruct((n, dm), BF16),
        compiler_params=_params(("parallel", "arbitrary")),
        name="merge",
    )(ya, yb, wpa, wpb, proj, proj)


def _outproj_kernel(m_ref, w_ref, x_ref, g_ref, h_ref, xnt_ref):
    h = x_ref[...] + jnp.dot(m_ref[...], w_ref[...], preferred_element_type=F32)
    h_ref[...] = h
    ms = jnp.mean(h * h, axis=-1, keepdims=True)
    xn = h * lax.rsqrt(ms + EPS) * g_ref[...]
    xnt_ref[...] = xn.T.astype(xnt_ref.dtype)


def _outproj(mixed, w_out, x, g, tm):
    n, d = x.shape
    return pl.pallas_call(
        _outproj_kernel,
        grid=(n // tm,),
        in_specs=[
            pl.BlockSpec((tm, d), lambda i: (i, 0)),
            pl.BlockSpec((d, d), lambda i: (0, 0)),
            pl.BlockSpec((tm, d), lambda i: (i, 0)),
            pl.BlockSpec((1, d), lambda i: (0, 0)),
        ],
        out_specs=[pl.BlockSpec((tm, d), lambda i: (i, 0)), pl.BlockSpec((d, tm), lambda i: (0, i))],
        out_shape=[jax.ShapeDtypeStruct((n, d), F32), jax.ShapeDtypeStruct((d, n), BF16)],
        compiler_params=_params(("parallel",)),
        name="outproj",
    )(mixed, w_out, x, g)


def _top_ranks(s, k):
    rank = jnp.full(s.shape, float(N_KEYS), F32)
    vals = []
    for j in range(k):
        m = jnp.max(s, axis=0, keepdims=True)
        hit = s == m
        rank = jnp.where(hit, float(j), rank)
        s = jnp.where(hit, -jnp.inf, s)
        vals.append(m)
    return rank, vals


def _retrieve_kernel(xnt_ref, wqt_ref, keys_ref, r2_ref, e2_ref, m_ref, w_ref):
    heads = keys_ref.shape[0]
    dq = keys_ref.shape[3]
    k = PEER_TOPK
    tm = xnt_ref.shape[1]
    qt = jnp.dot(wqt_ref[...], xnt_ref[...], preferred_element_type=F32).astype(BF16)
    for h in range(heads):
        s1_all = jnp.dot(keys_ref[h, 0], qt[(2 * h) * dq:(2 * h + 1) * dq], preferred_element_type=F32)
        s2_all = jnp.dot(keys_ref[h, 1], qt[(2 * h + 1) * dq:(2 * h + 2) * dq], preferred_element_type=F32)
        for lo in range(0, tm, LANES):
            cols = slice(lo, lo + LANES)
            s1, s2 = s1_all[:, cols], s2_all[:, cols]
            r1, t1 = _top_ranks(s1, k)
            r2, t2 = _top_ranks(s2, k)
            t2s = jnp.concatenate(t2, axis=0)
            pieces = [t1[i] + t2s[:k // (i + 1)] for i in range(k)]
            rows = sum(p.shape[0] for p in pieces)
            pieces.append(jnp.full((-rows % 8, LANES), -jnp.inf, F32))
            work = jnp.concatenate(pieces, axis=0)
            tau = None
            for _ in range(k):
                tau = jnp.max(work, axis=0, keepdims=True)
                work = jnp.where(work == tau, -jnp.inf, work)
            e2s = jnp.exp(t2s - t2[0])
            z = jnp.zeros_like(tau)
            m = jnp.zeros_like(s1)
            for i in range(k):
                sel = (t1[i] + t2s) >= tau
                cnt = jnp.sum(sel.astype(F32), axis=0, keepdims=True)
                z = z + jnp.exp(t1[i] - t1[0]) * jnp.sum(jnp.where(sel, e2s, 0.0), axis=0, keepdims=True)
                m = jnp.where(r1 == float(i), cnt, m)
            r2_ref[h, :, cols] = r2.astype(r2_ref.dtype)
            e2_ref[h, :, cols] = jnp.exp(s2 - t2[0]).astype(e2_ref.dtype)
            m_ref[h, :, cols] = m
            w_ref[h, :, cols] = (0.5 * jnp.exp(s1 - t1[0])) / z


def _retrieve(xnt, wqt, keys, tm):
    d, n = xnt.shape
    heads = keys.shape[0]
    assert tm % LANES == 0
    big = lambda: pl.BlockSpec((heads, N_KEYS, tm), lambda i: (0, 0, i))
    shape = lambda dt: jax.ShapeDtypeStruct((heads, N_KEYS, n), dt)
    return pl.pallas_call(
        _retrieve_kernel,
        grid=(n // tm,),
        in_specs=[
            pl.BlockSpec((d, tm), lambda i: (0, i)),
            pl.BlockSpec(wqt.shape, lambda i: (0, 0)),
            pl.BlockSpec(keys.shape, lambda i: (0, 0, 0, 0)),
        ],
        out_specs=[big(), big(), big(), big()],
        out_shape=[shape(BF16), shape(BF16), shape(F32), shape(F32)],
        compiler_params=_params(("parallel",)),
        name="peer_retrieve",
    )(xnt, wqt, keys)


def _experts_kernel(n_blk, xnt_ref, *refs):
    u_refs, v_refs = refs[:WEIGHT_SPLIT], refs[WEIGHT_SPLIT:2 * WEIGHT_SPLIT]
    r2_ref, e2_ref, m_ref, w_ref, h_ref, g_ref, o_ref, act0_ref, act1_ref, acc_ref = refs[2 * WEIGHT_SPLIT:]
    s = pl.program_id(0)
    heads = r2_ref.shape[0]
    ec = v_refs[0].shape[0]
    ur = u_refs[0].shape[0]
    vc = v_refs[0].shape[1]
    e_prev = jnp.maximum(s - 1, 0) % n_blk

    @pl.when(s == 0)
    def _():
        act1_ref[...] = jnp.zeros_like(act1_ref)

    @pl.when(e_prev == 0)
    def _():
        acc_ref[...] = jnp.zeros_like(acc_ref)

    def step(src_ref, dst_ref):
        for k, u_ref in enumerate(u_refs):
            dst_ref[k * ur:(k + 1) * ur, :] = jnp.dot(u_ref[...], xnt_ref[...], preferred_element_type=F32)
        ws = []
        for cc in range(ec // N_KEYS):
            rows = slice(cc * N_KEYS, (cc + 1) * N_KEYS)
            c = e_prev * (ec // N_KEYS) + cc
            gate = None
            for h in range(heads):
                mrow = m_ref[h, pl.ds(c, 1), :].astype(BF16)
                wrow = w_ref[h, pl.ds(c, 1), :].astype(BF16)
                gh = jnp.where(r2_ref[h] < mrow, e2_ref[h], 0) * wrow
                gate = gh if gate is None else gate + gh
            a = src_ref[rows, :]
            ge = a * (1.0 + lax.erf(a * (1.0 / math.sqrt(2.0))))
            ws.append(ge.astype(BF16) * gate)
        wt = jnp.concatenate(ws, axis=0).T
        for k, v_ref in enumerate(v_refs):
            acc_ref[:, k * vc:(k + 1) * vc] += jnp.dot(wt, v_ref[...], preferred_element_type=F32)

    @pl.when(s % 2 == 0)
    def _():
        step(act1_ref, act0_ref)

    @pl.when(s % 2 == 1)
    def _():
        step(act0_ref, act1_ref)

    @pl.when(jnp.logical_and(s > 0, e_prev == n_blk - 1))
    def _():
        hh = h_ref[...] + acc_ref[...]
        ms = jnp.mean(hh * hh, axis=-1, keepdims=True)
        o_ref[...] = hh * lax.rsqrt(ms + EPS) * g_ref[...]


def _experts(xnt, u, v, r2, e2, m, w, h2, g, tm, ec):
    d, n = xnt.shape
    n_exp = u.shape[0]
    heads = r2.shape[0]
    n_blk = n_exp // ec
    ns = WEIGHT_SPLIT
    assert ec % N_KEYS == 0
    steps = (n // tm) * n_blk
    cur = lambda s: jnp.minimum(s, steps - 1)
    prev = lambda s: jnp.maximum(s - 1, 0)
    tile_spec = lambda shape, imap: pl.BlockSpec(shape, imap, pipeline_mode=pl.Buffered(1))
    big = lambda: tile_spec((heads, N_KEYS, tm), lambda s: (0, 0, prev(s) // n_blk))
    return pl.pallas_call(
        functools.partial(_experts_kernel, n_blk),
        grid=(steps + 1,),
        in_specs=[
            pl.BlockSpec((d, tm), lambda s: (0, cur(s) // n_blk)),
            *[pl.BlockSpec((ec // ns, d), lambda s, k=k: ((cur(s) % n_blk) * ns + k, 0)) for k in range(ns)],
            *[pl.BlockSpec((ec, d // ns), lambda s, k=k: (prev(s) % n_blk, k)) for k in range(ns)],
            big(), big(), big(), big(),
            tile_spec((tm, d), lambda s: (prev(s) // n_blk, 0)),
            pl.BlockSpec((1, d), lambda s: (0, 0)),
        ],
        out_specs=pl.BlockSpec((tm, d), lambda s: (prev(s) // n_blk, 0)),
        out_shape=jax.ShapeDtypeStruct((n, d), F32),
        scratch_shapes=[pltpu.VMEM((ec, tm), F32), pltpu.VMEM((ec, tm), F32), pltpu.VMEM((tm, d), F32)],
        compiler_params=_params(("arbitrary",)),
        name="peer_experts",
    )(xnt, *([u] * ns), *([v] * ns), r2, e2, m, w, h2, g)


def _tile(n, want):
    t = min(n, want)
    assert n % t == 0, (n, t)
    return t


def kernel(x, meta, ln1_g, w_in, conv_w, conv_b, rg_wa, rg_ba, rg_wx, rg_bx, rg_lambda, hg_lb_logits,
           hg_norm_g, w_pa, w_pb, w_out, ln2_g, peer_wq, peer_keys, peer_u, peer_v, final_g):
    batch, seq, d = x.shape
    n = batch * seq
    n_meta = meta.shape[0]
    depth = w_in.shape[0]
    assert depth == 1 and seq % TT == 0 and n_meta <= TT
    rg_width = conv_w.shape[2]
    hg_width = hg_norm_g.shape[1]
    assert rg_width == hg_width == d and rg_wa.shape[2] == HEAD
    n_pad = TT - n_meta
    row = lambda a: a.reshape(1, -1)

    xf = x.reshape(n, d)
    meta_tile = jnp.concatenate([jnp.zeros((n_pad, d), x.dtype), meta.astype(x.dtype)], axis=0)

    w_in_b = w_in[0].astype(BF16)
    tm = _tile(n, 1024)
    proj = _norm_matmul(xf, row(ln1_g[0]), w_in_b, n + TT, 0, tm, 1024)
    proj = _norm_matmul(meta_tile, row(ln1_g[0]), w_in_b, n + TT, n // TT, TT, 1024, prev=proj)

    y_a = _rglru(proj, batch, seq, n_pad, conv_w[0], row(conv_b[0]), rg_wa[0].astype(BF16), row(rg_ba[0]),
                 rg_wx[0].astype(BF16), row(rg_bx[0]), row(rg_lambda[0]))
    y_b = _hgrn(proj, batch, seq, n_pad, 2, hg_lb_logits, row(hg_norm_g[0]))

    tn = 1024
    mixed = _merge(y_a, y_b, w_pa[0].astype(BF16), w_pb[0].astype(BF16), proj, 6 * (d // tn), tm, tn)
    h2, xn2t = _outproj(mixed, w_out[0].astype(BF16), xf, row(ln2_g[0]), _tile(n, 512))

    tp = _tile(n, 512)
    r2, e2, m, w = _retrieve(xn2t, peer_wq[0].T.astype(BF16), peer_keys[0].astype(BF16), tp)
    out = _experts(xn2t, peer_u[0].astype(BF16), peer_v[0].astype(BF16), r2, e2, m, w, h2,
                   row(final_g), tp, 1024)
    return out.reshape(batch, seq, d)
```

```python
import functools
import math

import jax
import jax.numpy as jnp
from jax import lax
from jax.experimental import pallas as pl
from jax.experimental.pallas import tpu as pltpu

F32 = jnp.float32
BF16 = jnp.bfloat16

EPS = 1e-6
RG_C = 8.0
CONV_WIDTH = 4
HEAD = 128
N_KEYS = 128
PEER_TOPK = 16
TT = 128
RB = 32
LANES = 128
EXPERT_GROUP = 512
VMEM_LIMIT = 56 * 1024 * 1024

NT_DIMS = (((1,), (1,)), ((), ()))
TN_DIMS = (((0,), (0,)), ((), ()))


def _params(sem):
    return pltpu.CompilerParams(dimension_semantics=sem, vmem_limit_bytes=VMEM_LIMIT)


def _sigmoid(x):
    return jax.nn.sigmoid(x)


def _gelu(x):
    return 0.5 * x * (1.0 + lax.erf(x * (1.0 / math.sqrt(2.0))))


def _norm_mm_kernel(x_ref, g_ref, w_ref, *rest):
    o_ref, xn_ref = rest[-2], rest[-1]

    @pl.when(pl.program_id(1) == 0)
    def _():
        xf = x_ref[...]
        ms = jnp.mean(xf * xf, axis=-1, keepdims=True)
        xn_ref[...] = (xf * lax.rsqrt(ms + EPS) * g_ref[...]).astype(BF16)

    o_ref[...] = jnp.dot(xn_ref[...], w_ref[...], preferred_element_type=F32).astype(o_ref.dtype)


def _norm_matmul(x, g, w, out_rows, row_block_offset, tm, tn, prev=None):
    m, d = x.shape
    n = w.shape[1]
    in_specs = [
        pl.BlockSpec((tm, d), lambda i, j: (i, 0)),
        pl.BlockSpec((1, d), lambda i, j: (0, 0)),
        pl.BlockSpec((d, tn), lambda i, j: (0, j)),
    ]
    args = [x, g, w]
    aliases = {}
    if prev is not None:
        in_specs.append(pl.BlockSpec(memory_space=pl.ANY))
        args.append(prev)
        aliases = {3: 0}
    return pl.pallas_call(
        _norm_mm_kernel,
        grid=(m // tm, n // tn),
        in_specs=in_specs,
        out_specs=pl.BlockSpec((tm, tn), lambda i, j: (i + row_block_offset, j)),
        out_shape=jax.ShapeDtypeStruct((out_rows, n), BF16),
        scratch_shapes=[pltpu.VMEM((tm, d), BF16)],
        input_output_aliases=aliases,
        compiler_params=_params(("parallel", "arbitrary")),
        name="norm_matmul",
    )(*args)


def _rglru_kernel(n_pad, xa_ref, ya_ref, cw_ref, cb_ref, wa_ref, ba_ref, wx_ref, bx_ref, lam_ref,
                  o_ref, xbuf_ref, hc_ref):
    t = pl.program_id(1)
    width = xa_ref.shape[1]

    @pl.when(t == 0)
    def _():
        xbuf_ref[0:8, :] = jnp.zeros((8, width), F32)
        hc_ref[...] = jnp.zeros_like(hc_ref)

    xbuf_ref[8:8 + TT, :] = xa_ref[...].astype(F32)
    row = lax.broadcasted_iota(jnp.int32, (TT, HEAD), 0)
    live = jnp.logical_or(t > 0, row >= n_pad)
    r8 = row & 7

    for h in range(width // HEAD):
        sl = slice(h * HEAD, (h + 1) * HEAD)
        xc = cb_ref[:, sl] + cw_ref[0:1, sl] * xbuf_ref[5:5 + TT, sl]
        for k in range(1, CONV_WIDTH):
            xc = xc + cw_ref[k:k + 1, sl] * xbuf_ref[5 + k:5 + k + TT, sl]
        xcb = xc.astype(BF16)
        r = _sigmoid(jnp.dot(xcb, wa_ref[h], preferred_element_type=F32) + ba_ref[:, sl])
        i = _sigmoid(jnp.dot(xcb, wx_ref[h], preferred_element_type=F32) + bx_ref[:, sl])
        sp = jax.nn.softplus(-lam_ref[:, sl])
        log_a = (-RG_C) * r * sp
        a = jnp.exp(log_a)
        th = jnp.tanh(log_a)
        u = jnp.sqrt(-2.0 * th / (1.0 - th)) * (i * xc)
        u = jnp.where(live, u, 0.0)
        for k in (1, 2, 4):
            a_sh = pltpu.roll(a, k, 0)
            u_sh = pltpu.roll(u, k, 0)
            m = r8 >= k
            u = jnp.where(m, a * u_sh + u, u)
            a = jnp.where(m, a * a_sh, a)
        carry = hc_ref[:, sl]
        outs = []
        for g in range(TT // 8):
            hg = a[g * 8:(g + 1) * 8] * carry + u[g * 8:(g + 1) * 8]
            outs.append(hg)
            carry = hg[7:8]
        hc_ref[:, sl] = carry
        hs = jnp.concatenate(outs, axis=0)
        o_ref[:, sl] = (_gelu(ya_ref[:, sl].astype(F32)) * hs).astype(o_ref.dtype)

    xbuf_ref[0:8, :] = xbuf_ref[TT:TT + 8, :]


def _rglru(proj, batch, seq, n_pad, conv_w, conv_b, wa, ba, wx, bx, lam):
    n = batch * seq
    width = conv_w.shape[1]
    tiles = seq // TT
    meta_blk = n // TT

    def in_row(b, t):
        return jnp.where(t == 0, meta_blk, b * tiles + t - 1)

    vec = lambda: pl.BlockSpec((1, width), lambda b, t: (0, 0))
    gate_w = lambda: pl.BlockSpec((width // HEAD, HEAD, HEAD), lambda b, t: (0, 0, 0))
    return pl.pallas_call(
        functools.partial(_rglru_kernel, n_pad),
        grid=(batch, tiles + 1),
        in_specs=[
            pl.BlockSpec((TT, width), lambda b, t: (in_row(b, t), 0)),
            pl.BlockSpec((TT, width), lambda b, t: (in_row(b, t), 1)),
            pl.BlockSpec((CONV_WIDTH, width), lambda b, t: (0, 0)),
            vec(), gate_w(), vec(), gate_w(), vec(), vec(),
        ],
        out_specs=pl.BlockSpec((TT, width), lambda b, t: (b * tiles + jnp.maximum(t - 1, 0), 0)),
        out_shape=jax.ShapeDtypeStruct((n, width), BF16),
        scratch_shapes=[pltpu.VMEM((TT + 8, width), F32), pltpu.VMEM((1, width), F32)],
        compiler_params=_params(("parallel", "arbitrary")),
        name="rglru",
    )(proj, proj, conv_w, conv_b, wa, ba, wx, bx, lam)


def _hgrn_kernel(n_pad, q_ref, f_ref, v_ref, g_ref, lbl_ref, ng_ref, o_ref, st_ref):
    t = pl.program_id(1)
    width = q_ref.shape[1]

    @pl.when(t == 0)
    def _():
        st_ref[...] = jnp.zeros_like(st_ref)

    heads = [slice(h * HEAD, (h + 1) * HEAD) for h in range(width // HEAD)]
    blocks = [(r0, r0 + RB, r0 + RB // 2 - 1) for r0 in range(0, TT, RB)]

    row = lax.broadcasted_iota(jnp.int32, (TT, width), 0)
    live = jnp.logical_or(t > 0, row >= n_pad)
    lg = lbl_ref[...]
    e = jnp.exp(lg - jnp.max(lg, axis=0, keepdims=True))
    lb = e[0:1] / jnp.sum(e, axis=0, keepdims=True)
    f = lb + (1.0 - lb) * _sigmoid(f_ref[...].astype(F32))
    kk = 1.0 - f
    bcum = jnp.where(live, jnp.log(f), 0.0)
    k = 1
    while k < TT:
        bcum = bcum + jnp.where(row >= k, pltpu.roll(bcum, k, 0), 0.0)
        k *= 2
    bend = bcum[TT - 1:TT]
    q = q_ref[...].astype(F32)
    qs = q * _sigmoid(q)
    qhat = (qs * jnp.exp(bcum)).astype(BF16)
    khat = (kk * jnp.exp(bend - bcum)).astype(BF16)
    dec = jnp.exp(bend)
    qts, kts = [], []
    for r0, r1, mid in blocks:
        bref = bcum[mid:mid + 1]
        qts.append((qs[r0:r1] * jnp.exp(bcum[r0:r1] - bref)).astype(BF16))
        kt = kk * jnp.exp(bref - bcum)
        kts.append((kt if r1 == TT else jnp.where(row < r1, kt, 0.0)).astype(BF16))

    tri = lax.broadcasted_iota(jnp.int32, (TT, TT), 0) >= lax.broadcasted_iota(jnp.int32, (TT, TT), 1)
    o_inter = [lax.dot_general(qhat[:, sl], st_ref[h].astype(BF16), NT_DIMS, preferred_element_type=F32)
               for h, sl in enumerate(heads)]
    scores = [[lax.dot_general(qt[:, sl], kt[:, sl], NT_DIMS, preferred_element_type=F32)
               for qt, kt in zip(qts, kts)] for sl in heads]
    outs = []
    for h, sl in enumerate(heads):
        p = jnp.where(tri, jnp.concatenate(scores[h], axis=0), 0.0).astype(BF16)
        outs.append(jnp.dot(p, v_ref[:, sl], preferred_element_type=F32) + o_inter[h])
    for h, sl in enumerate(heads):
        st_ref[h] = st_ref[h] * dec[:, sl] + lax.dot_general(v_ref[:, sl], khat[:, sl], TN_DIMS,
                                                             preferred_element_type=F32)
    for h, sl in enumerate(heads):
        o = outs[h]
        o = o * lax.rsqrt(jnp.mean(o * o, axis=-1, keepdims=True) + EPS)
        og = g_ref[:, sl].astype(F32)
        o_ref[:, sl] = (o * ng_ref[:, sl] * (og * _sigmoid(og))).astype(o_ref.dtype)


def _hgrn(proj, batch, seq, n_pad, col0, lb_logits, norm_g):
    n = batch * seq
    width = norm_g.shape[1]
    tiles = seq // TT
    meta_blk = n // TT

    def in_row(b, t):
        return jnp.where(t == 0, meta_blk, b * tiles + t - 1)

    col = lambda c: pl.BlockSpec((TT, width), lambda b, t: (in_row(b, t), col0 + c))
    return pl.pallas_call(
        functools.partial(_hgrn_kernel, n_pad),
        grid=(batch, tiles + 1),
        in_specs=[
            col(0), col(1), col(2), col(3),
            pl.BlockSpec(lb_logits.shape, lambda b, t: (0, 0)),
            pl.BlockSpec((1, width), lambda b, t: (0, 0)),
        ],
        out_specs=pl.BlockSpec((TT, width), lambda b, t: (b * tiles + jnp.maximum(t - 1, 0), 0)),
        out_shape=jax.ShapeDtypeStruct((n, width), BF16),
        scratch_shapes=[pltpu.VMEM((width // HEAD, HEAD, HEAD), F32)],
        compiler_params=_params(("parallel", "arbitrary")),
        name="hgrn2",
    )(proj, proj, proj, proj, lb_logits, norm_g)


def _merge_kernel(ya_ref, yb_ref, wpa_ref, wpb_ref, za_ref, zb_ref, o_ref):
    ta = jnp.dot(ya_ref[...], wpa_ref[...], preferred_element_type=F32)
    tb = jnp.dot(yb_ref[...], wpb_ref[...], preferred_element_type=F32)
    mixed = _sigmoid(za_ref[...].astype(F32)) * ta + _sigmoid(zb_ref[...].astype(F32)) * tb
    o_ref[...] = mixed.astype(o_ref.dtype)


def _merge(ya, yb, wpa, wpb, proj, zcol0, tm, tn):
    n, d = ya.shape
    dm = wpa.shape[1]
    nj = dm // tn
    return pl.pallas_call(
        _merge_kernel,
        grid=(n // tm, nj),
        in_specs=[
            pl.BlockSpec((tm, d), lambda i, j: (i, 0)),
            pl.BlockSpec((tm, d), lambda i, j: (i, 0)),
            pl.BlockSpec((d, tn), lambda i, j: (0, j)),
            pl.BlockSpec((d, tn), lambda i, j: (0, j)),
            pl.BlockSpec((tm, tn), lambda i, j: (i, zcol0 + j)),
            pl.BlockSpec((tm, tn), lambda i, j: (i, zcol0 + nj + j)),
        ],
        out_specs=pl.BlockSpec((tm, tn), lambda i, j: (i, j)),
        out_shape=jax.ShapeDtypeStruct((n, dm), BF16),
        compiler_params=_params(("parallel", "arbitrary")),
        name="merge",
    )(ya, yb, wpa, wpb, proj, proj)


def _outproj_kernel(m_ref, w_ref, x_ref, g_ref, h_ref, xnt_ref):
    h = x_ref[...] + jnp.dot(m_ref[...], w_ref[...], preferred_element_type=F32)
    h_ref[...] = h
    ms = jnp.mean(h * h, axis=-1, keepdims=True)
    xn = h * lax.rsqrt(ms + EPS) * g_ref[...]
    xnt_ref[...] = xn.T.astype(xnt_ref.dtype)


def _outproj(mixed, w_out, x, g, tm):
    n, d = x.shape
    return pl.pallas_call(
        _outproj_kernel,
        grid=(n // tm,),
        in_specs=[
            pl.BlockSpec((tm, d), lambda i: (i, 0)),
            pl.BlockSpec((d, d), lambda i: (0, 0)),
            pl.BlockSpec((tm, d), lambda i: (i, 0)),
            pl.BlockSpec((1, d), lambda i: (0, 0)),
        ],
        out_specs=[pl.BlockSpec((tm, d), lambda i: (i, 0)), pl.BlockSpec((d, tm), lambda i: (0, i))],
        out_shape=[jax.ShapeDtypeStruct((n, d), F32), jax.ShapeDtypeStruct((d, n), BF16)],
        compiler_params=_params(("parallel",)),
        name="outproj",
    )(mixed, w_out, x, g)


def _top_ranks(s, k):
    rank = jnp.full(s.shape, float(N_KEYS), F32)
    vals = []
    for j in range(k):
        m = jnp.max(s, axis=0, keepdims=True)
        hit = s == m
        rank = jnp.where(hit, float(j), rank)
        s = jnp.where(hit, -jnp.inf, s)
        vals.append(m)
    return rank, vals


def _retrieve_kernel(xnt_ref, wqt_ref, keys_ref, r2_ref, e2_ref, m_ref, w_ref):
    heads = keys_ref.shape[0]
    dq = keys_ref.shape[3]
    k = PEER_TOPK
    tm = xnt_ref.shape[1]
    qt = jnp.dot(wqt_ref[...], xnt_ref[...], preferred_element_type=F32).astype(BF16)
    for h in range(heads):
        s1_all = jnp.dot(keys_ref[h, 0], qt[(2 * h) * dq:(2 * h + 1) * dq], preferred_element_type=F32)
        s2_all = jnp.dot(keys_ref[h, 1], qt[(2 * h + 1) * dq:(2 * h + 2) * dq], preferred_element_type=F32)
        for lo in range(0, tm, LANES):
            cols = slice(lo, lo + LANES)
            s1, s2 = s1_all[:, cols], s2_all[:, cols]
            r1, t1 = _top_ranks(s1, k)
            r2, t2 = _top_ranks(s2, k)
            t2s = jnp.concatenate(t2, axis=0)
            pieces = [t1[i] + t2s[:k // (i + 1)] for i in range(k)]
            rows = sum(p.shape[0] for p in pieces)
            pieces.append(jnp.full((-rows % 8, LANES), -jnp.inf, F32))
            work = jnp.concatenate(pieces, axis=0)
            tau = None
            for _ in range(k):
                tau = jnp.max(work, axis=0, keepdims=True)
                work = jnp.where(work == tau, -jnp.inf, work)
            e2s = jnp.exp(t2s - t2[0])
            z = jnp.zeros_like(tau)
            m = jnp.zeros_like(s1)
            for i in range(k):
                sel = (t1[i] + t2s) >= tau
                cnt = jnp.sum(sel.astype(F32), axis=0, keepdims=True)
                z = z + jnp.exp(t1[i] - t1[0]) * jnp.sum(jnp.where(sel, e2s, 0.0), axis=0, keepdims=True)
                m = jnp.where(r1 == float(i), cnt, m)
            r2_ref[h, :, cols] = r2.astype(r2_ref.dtype)
            e2_ref[h, :, cols] = jnp.exp(s2 - t2[0]).astype(e2_ref.dtype)
            m_ref[h, :, cols] = m
            w_ref[h, :, cols] = (0.5 * jnp.exp(s1 - t1[0])) / z


def _retrieve(xnt, wqt, keys, tm):
    d, n = xnt.shape
    heads = keys.shape[0]
    assert tm % LANES == 0
    big = lambda: pl.BlockSpec((heads, N_KEYS, tm), lambda i: (0, 0, i))
    shape = lambda dt: jax.ShapeDtypeStruct((heads, N_KEYS, n), dt)
    return pl.pallas_call(
        _retrieve_kernel,
        grid=(n // tm,),
        in_specs=[
            pl.BlockSpec((d, tm), lambda i: (0, i)),
            pl.BlockSpec(wqt.shape, lambda i: (0, 0)),
            pl.BlockSpec(keys.shape, lambda i: (0, 0, 0, 0)),
        ],
        out_specs=[big(), big(), big(), big()],
        out_shape=[shape(BF16), shape(BF16), shape(F32), shape(F32)],
        compiler_params=_params(("parallel",)),
        name="peer_retrieve",
    )(xnt, wqt, keys)


def _experts_kernel(n_blk, xnt_ref, u_ref, v_ref, r2_ref, e2_ref, m_ref, w_ref, h_ref, g_ref, o_ref,
                    act0_ref, act1_ref, acc_ref):
    s = pl.program_id(0)
    heads = r2_ref.shape[0]
    ec = u_ref.shape[0]
    e_prev = jnp.maximum(s - 1, 0) % n_blk

    @pl.when(s == 0)
    def _():
        act1_ref[...] = jnp.zeros_like(act1_ref)

    @pl.when(e_prev == 0)
    def _():
        acc_ref[...] = jnp.zeros_like(acc_ref)

    def step(src_ref, dst_ref):
        dst_ref[...] = jnp.dot(u_ref[...], xnt_ref[...], preferred_element_type=F32)
        for k0 in range(0, ec, EXPERT_GROUP):
            ws = []
            for r0 in range(k0, k0 + EXPERT_GROUP, N_KEYS):
                rows = slice(r0, r0 + N_KEYS)
                c = (e_prev * ec + r0) // N_KEYS
                gate = None
                for h in range(heads):
                    mrow = m_ref[h, pl.ds(c, 1), :].astype(BF16)
                    wrow = w_ref[h, pl.ds(c, 1), :].astype(BF16)
                    gh = jnp.where(r2_ref[h] < mrow, e2_ref[h], 0) * wrow
                    gate = gh if gate is None else gate + gh
                a = src_ref[rows, :]
                ge = a * (1.0 + lax.erf(a * (1.0 / math.sqrt(2.0))))
                ws.append(ge.astype(BF16) * gate)
            wt = jnp.concatenate(ws, axis=0).T
            acc_ref[...] += jnp.dot(wt, v_ref[k0:k0 + EXPERT_GROUP, :], preferred_element_type=F32)

    @pl.when(s % 2 == 0)
    def _():
        step(act1_ref, act0_ref)

    @pl.when(s % 2 == 1)
    def _():
        step(act0_ref, act1_ref)

    @pl.when(jnp.logical_and(s > 0, e_prev == n_blk - 1))
    def _():
        hh = h_ref[...] + acc_ref[...]
        ms = jnp.mean(hh * hh, axis=-1, keepdims=True)
        o_ref[...] = hh * lax.rsqrt(ms + EPS) * g_ref[...]


def _experts(xnt, u, v, r2, e2, m, w, h2, g, tm, ec):
    d, n = xnt.shape
    n_exp = u.shape[0]
    heads = r2.shape[0]
    n_blk = n_exp // ec
    assert ec % EXPERT_GROUP == 0 and EXPERT_GROUP % N_KEYS == 0
    steps = (n // tm) * n_blk
    cur = lambda s: jnp.minimum(s, steps - 1)
    prev = lambda s: jnp.maximum(s - 1, 0)
    tile_spec = lambda shape, imap: pl.BlockSpec(shape, imap, pipeline_mode=pl.Buffered(1))
    big = lambda: tile_spec((heads, N_KEYS, tm), lambda s: (0, 0, prev(s) // n_blk))
    return pl.pallas_call(
        functools.partial(_experts_kernel, n_blk),
        grid=(steps + 1,),
        in_specs=[
            pl.BlockSpec((d, tm), lambda s: (0, cur(s) // n_blk)),
            pl.BlockSpec((ec, d), lambda s: (cur(s) % n_blk, 0)),
            pl.BlockSpec((ec, d), lambda s: (prev(s) % n_blk, 0)),
            big(), big(), big(), big(),
            tile_spec((tm, d), lambda s: (prev(s) // n_blk, 0)),
            pl.BlockSpec((1, d), lambda s: (0, 0)),
        ],
        out_specs=pl.BlockSpec((tm, d), lambda s: (prev(s) // n_blk, 0)),
        out_shape=jax.ShapeDtypeStruct((n, d), F32),
        scratch_shapes=[pltpu.VMEM((ec, tm), F32), pltpu.VMEM((ec, tm), F32), pltpu.VMEM((tm, d), F32)],
        compiler_params=_params(("arbitrary",)),
        name="peer_experts",
    )(xnt, u, v, r2, e2, m, w, h2, g)


def _tile(n, want):
    t = min(n, want)
    assert n % t == 0, (n, t)
    return t


def kernel(x, meta, ln1_g, w_in, conv_w, conv_b, rg_wa, rg_ba, rg_wx, rg_bx, rg_lambda, hg_lb_logits,
           hg_norm_g, w_pa, w_pb, w_out, ln2_g, peer_wq, peer_keys, peer_u, peer_v, final_g):
    batch, seq, d = x.shape
    n = batch * seq
    n_meta = meta.shape[0]
    depth = w_in.shape[0]
    assert depth == 1 and seq % TT == 0 and n_meta <= TT
    rg_width = conv_w.shape[2]
    hg_width = hg_norm_g.shape[1]
    assert rg_width == hg_width == d and rg_wa.shape[2] == HEAD
    n_pad = TT - n_meta
    row = lambda a: a.reshape(1, -1)

    xf = x.reshape(n, d)
    meta_tile = jnp.concatenate([jnp.zeros((n_pad, d), x.dtype), meta.astype(x.dtype)], axis=0)

    w_in_b = w_in[0].astype(BF16)
    tm = _tile(n, 1024)
    proj = _norm_matmul(xf, row(ln1_g[0]), w_in_b, n + TT, 0, tm, 1024)
    proj = _norm_matmul(meta_tile, row(ln1_g[0]), w_in_b, n + TT, n // TT, TT, 1024, prev=proj)

    y_a = _rglru(proj, batch, seq, n_pad, conv_w[0], row(conv_b[0]), rg_wa[0].astype(BF16), row(rg_ba[0]),
                 rg_wx[0].astype(BF16), row(rg_bx[0]), row(rg_lambda[0]))
    y_b = _hgrn(proj, batch, seq, n_pad, 2, hg_lb_logits, row(hg_norm_g[0]))

    tn = 1024
    mixed = _merge(y_a, y_b, w_pa[0].astype(BF16), w_pb[0].astype(BF16), proj, 6 * (d // tn), tm, tn)
    h2, xn2t = _outproj(mixed, w_out[0].astype(BF16), xf, row(ln2_g[0]), _tile(n, 512))

    tp = _tile(n, 512)
    r2, e2, m, w = _retrieve(xn2t, peer_wq[0].T.astype(BF16), peer_keys[0].astype(BF16), tp)
    out = _experts(xn2t, peer_u[0].astype(BF16), peer_v[0].astype(BF16), r2, e2, m, w, h2,
                   row(final_g), tp, 1024)
    return out.reshape(batch, seq, d)
```

```python
import functools
import math

import jax
import jax.numpy as jnp
from jax import lax
from jax.experimental import pallas as pl
from jax.experimental.pallas import tpu as pltpu

F32 = jnp.float32
BF16 = jnp.bfloat16

EPS = 1e-6
RG_C = 8.0
CONV_WIDTH = 4
HEAD = 128
N_KEYS = 128
PEER_TOPK = 16
TT = 128
RB = 32
LANES = 128
EXPERT_GROUP = 1024
VMEM_LIMIT = 56 * 1024 * 1024

NT_DIMS = (((1,), (1,)), ((), ()))
TN_DIMS = (((0,), (0,)), ((), ()))


def _params(sem):
    return pltpu.CompilerParams(dimension_semantics=sem, vmem_limit_bytes=VMEM_LIMIT)


def _sigmoid(x):
    return jax.nn.sigmoid(x)


def _gelu(x):
    return 0.5 * x * (1.0 + lax.erf(x * (1.0 / math.sqrt(2.0))))


def _norm_mm_kernel(x_ref, g_ref, w_ref, o_ref, xn_ref):
    @pl.when(pl.program_id(1) == 0)
    def _():
        xf = x_ref[...]
        ms = jnp.mean(xf * xf, axis=-1, keepdims=True)
        xn_ref[...] = (xf * lax.rsqrt(ms + EPS) * g_ref[...]).astype(BF16)

    o_ref[...] = jnp.dot(xn_ref[...], w_ref[...], preferred_element_type=F32).astype(o_ref.dtype)


def _norm_matmul(x, g, w, tm, tn):
    m, d = x.shape
    n = w.shape[1]
    return pl.pallas_call(
        _norm_mm_kernel,
        grid=(m // tm, n // tn),
        in_specs=[
            pl.BlockSpec((tm, d), lambda i, j: (i, 0)),
            pl.BlockSpec((1, d), lambda i, j: (0, 0)),
            pl.BlockSpec((d, tn), lambda i, j: (0, j)),
        ],
        out_specs=pl.BlockSpec((tm, tn), lambda i, j: (i, j)),
        out_shape=jax.ShapeDtypeStruct((m, n), BF16),
        scratch_shapes=[pltpu.VMEM((tm, d), BF16)],
        compiler_params=_params(("parallel", "arbitrary")),
        name="norm_matmul",
    )(x, g, w)


def _rglru_kernel(n_pad, xa_ref, ya_ref, xam_ref, yam_ref, cw_ref, cb_ref, wa_ref, ba_ref, wx_ref, bx_ref,
                  lam_ref, o_ref, xbuf_ref, hc_ref):
    t = pl.program_id(1)
    width = xa_ref.shape[1]
    is_meta = t == 0

    @pl.when(is_meta)
    def _():
        xbuf_ref[0:8, :] = jnp.zeros((8, width), F32)
        hc_ref[...] = jnp.zeros_like(hc_ref)

    xbuf_ref[8:8 + TT, :] = jnp.where(is_meta, xam_ref[...], xa_ref[...]).astype(F32)
    row = lax.broadcasted_iota(jnp.int32, (TT, HEAD), 0)
    live = jnp.logical_or(t > 0, row >= n_pad)
    r8 = row & 7

    for h in range(width // HEAD):
        sl = slice(h * HEAD, (h + 1) * HEAD)
        xc = cb_ref[:, sl] + cw_ref[0:1, sl] * xbuf_ref[5:5 + TT, sl]
        for k in range(1, CONV_WIDTH):
            xc = xc + cw_ref[k:k + 1, sl] * xbuf_ref[5 + k:5 + k + TT, sl]
        xcb = xc.astype(BF16)
        r = _sigmoid(jnp.dot(xcb, wa_ref[h], preferred_element_type=F32) + ba_ref[:, sl])
        i = _sigmoid(jnp.dot(xcb, wx_ref[h], preferred_element_type=F32) + bx_ref[:, sl])
        sp = jax.nn.softplus(-lam_ref[:, sl])
        log_a = (-RG_C) * r * sp
        a = jnp.exp(log_a)
        th = jnp.tanh(log_a)
        u = jnp.sqrt(-2.0 * th / (1.0 - th)) * (i * xc)
        u = jnp.where(live, u, 0.0)
        for k in (1, 2, 4):
            a_sh = pltpu.roll(a, k, 0)
            u_sh = pltpu.roll(u, k, 0)
            m = r8 >= k
            u = jnp.where(m, a * u_sh + u, u)
            a = jnp.where(m, a * a_sh, a)
        carry = hc_ref[:, sl]
        outs = []
        for g in range(TT // 8):
            hg = a[g * 8:(g + 1) * 8] * carry + u[g * 8:(g + 1) * 8]
            outs.append(hg)
            carry = hg[7:8]
        hc_ref[:, sl] = carry
        hs = jnp.concatenate(outs, axis=0)
        ya = jnp.where(is_meta, yam_ref[:, sl], ya_ref[:, sl]).astype(F32)
        o_ref[:, sl] = (_gelu(ya) * hs).astype(o_ref.dtype)

    xbuf_ref[0:8, :] = xbuf_ref[TT:TT + 8, :]


def _rglru(proj, proj_meta, batch, seq, n_pad, conv_w, conv_b, wa, ba, wx, bx, lam):
    n = batch * seq
    width = conv_w.shape[1]
    tiles = seq // TT
    in_row = lambda b, t: b * tiles + jnp.maximum(t - 1, 0)

    vec = lambda: pl.BlockSpec((1, width), lambda b, t: (0, 0))
    gate_w = lambda: pl.BlockSpec((width // HEAD, HEAD, HEAD), lambda b, t: (0, 0, 0))
    return pl.pallas_call(
        functools.partial(_rglru_kernel, n_pad),
        grid=(batch, tiles + 1),
        in_specs=[
            pl.BlockSpec((TT, width), lambda b, t: (in_row(b, t), 0)),
            pl.BlockSpec((TT, width), lambda b, t: (in_row(b, t), 1)),
            pl.BlockSpec((TT, width), lambda b, t: (0, 0)),
            pl.BlockSpec((TT, width), lambda b, t: (0, 1)),
            pl.BlockSpec((CONV_WIDTH, width), lambda b, t: (0, 0)),
            vec(), gate_w(), vec(), gate_w(), vec(), vec(),
        ],
        out_specs=pl.BlockSpec((TT, width), lambda b, t: (in_row(b, t), 0)),
        out_shape=jax.ShapeDtypeStruct((n, width), BF16),
        scratch_shapes=[pltpu.VMEM((TT + 8, width), F32), pltpu.VMEM((1, width), F32)],
        compiler_params=_params(("parallel", "arbitrary")),
        name="rglru",
    )(proj, proj, proj_meta, proj_meta, conv_w, conv_b, wa, ba, wx, bx, lam)


def _hgrn_kernel(n_pad, q_ref, f_ref, v_ref, g_ref, qm_ref, fm_ref, vm_ref, gm_ref, lbl_ref, ng_ref, o_ref,
                 st_ref):
    t = pl.program_id(1)
    width = q_ref.shape[1]
    is_meta = t == 0

    @pl.when(is_meta)
    def _():
        st_ref[...] = jnp.zeros_like(st_ref)

    pick = lambda meta_ref, ref: jnp.where(is_meta, meta_ref[...], ref[...])
    q_in, f_in, v_in, g_in = pick(qm_ref, q_ref), pick(fm_ref, f_ref), pick(vm_ref, v_ref), pick(gm_ref, g_ref)

    heads = [slice(h * HEAD, (h + 1) * HEAD) for h in range(width // HEAD)]
    blocks = [(r0, r0 + RB, r0 + RB // 2 - 1) for r0 in range(0, TT, RB)]

    row = lax.broadcasted_iota(jnp.int32, (TT, width), 0)
    live = jnp.logical_or(t > 0, row >= n_pad)
    lg = lbl_ref[...]
    e = jnp.exp(lg - jnp.max(lg, axis=0, keepdims=True))
    lb = e[0:1] / jnp.sum(e, axis=0, keepdims=True)
    f = lb + (1.0 - lb) * _sigmoid(f_in.astype(F32))
    kk = 1.0 - f
    bcum = jnp.where(live, jnp.log(f), 0.0)
    k = 1
    while k < TT:
        bcum = bcum + jnp.where(row >= k, pltpu.roll(bcum, k, 0), 0.0)
        k *= 2
    bend = bcum[TT - 1:TT]
    q = q_in.astype(F32)
    qs = q * _sigmoid(q)
    qhat = (qs * jnp.exp(bcum)).astype(BF16)
    khat = (kk * jnp.exp(bend - bcum)).astype(BF16)
    dec = jnp.exp(bend)
    qts, kts = [], []
    for r0, r1, mid in blocks:
        bref = bcum[mid:mid + 1]
        qts.append((qs[r0:r1] * jnp.exp(bcum[r0:r1] - bref)).astype(BF16))
        kt = kk * jnp.exp(bref - bcum)
        kts.append((kt if r1 == TT else jnp.where(row < r1, kt, 0.0)).astype(BF16))

    tri = lax.broadcasted_iota(jnp.int32, (TT, TT), 0) >= lax.broadcasted_iota(jnp.int32, (TT, TT), 1)
    o_inter = [lax.dot_general(qhat[:, sl], st_ref[h].astype(BF16), NT_DIMS, preferred_element_type=F32)
               for h, sl in enumerate(heads)]
    scores = [[lax.dot_general(qt[:, sl], kt[:, sl], NT_DIMS, preferred_element_type=F32)
               for qt, kt in zip(qts, kts)] for sl in heads]
    outs = []
    for h, sl in enumerate(heads):
        p = jnp.where(tri, jnp.concatenate(scores[h], axis=0), 0.0).astype(BF16)
        outs.append(jnp.dot(p, v_in[:, sl], preferred_element_type=F32) + o_inter[h])
    for h, sl in enumerate(heads):
        st_ref[h] = st_ref[h] * dec[:, sl] + lax.dot_general(v_in[:, sl], khat[:, sl], TN_DIMS,
                                                             preferred_element_type=F32)
    for h, sl in enumerate(heads):
        o = outs[h]
        o = o * lax.rsqrt(jnp.mean(o * o, axis=-1, keepdims=True) + EPS)
        og = g_in[:, sl].astype(F32)
        o_ref[:, sl] = (o * ng_ref[:, sl] * (og * _sigmoid(og))).astype(o_ref.dtype)


def _hgrn(proj, proj_meta, batch, seq, n_pad, col0, lb_logits, norm_g):
    n = batch * seq
    width = norm_g.shape[1]
    tiles = seq // TT
    in_row = lambda b, t: b * tiles + jnp.maximum(t - 1, 0)

    col = lambda c: pl.BlockSpec((TT, width), lambda b, t: (in_row(b, t), col0 + c))
    meta_col = lambda c: pl.BlockSpec((TT, width), lambda b, t: (0, col0 + c))
    return pl.pallas_call(
        functools.partial(_hgrn_kernel, n_pad),
        grid=(batch, tiles + 1),
        in_specs=[
            col(0), col(1), col(2), col(3),
            meta_col(0), meta_col(1), meta_col(2), meta_col(3),
            pl.BlockSpec(lb_logits.shape, lambda b, t: (0, 0)),
            pl.BlockSpec((1, width), lambda b, t: (0, 0)),
        ],
        out_specs=pl.BlockSpec((TT, width), lambda b, t: (in_row(b, t), 0)),
        out_shape=jax.ShapeDtypeStruct((n, width), BF16),
        scratch_shapes=[pltpu.VMEM((width // HEAD, HEAD, HEAD), F32)],
        compiler_params=_params(("parallel", "arbitrary")),
        name="hgrn2",
    )(proj, proj, proj, proj, proj_meta, proj_meta, proj_meta, proj_meta, lb_logits, norm_g)


def _merge_kernel(ya_ref, yb_ref, wpa_ref, wpb_ref, za_ref, zb_ref, o_ref):
    ta = jnp.dot(ya_ref[...], wpa_ref[...], preferred_element_type=F32)
    tb = jnp.dot(yb_ref[...], wpb_ref[...], preferred_element_type=F32)
    mixed = _sigmoid(za_ref[...].astype(F32)) * ta + _sigmoid(zb_ref[...].astype(F32)) * tb
    o_ref[...] = mixed.astype(o_ref.dtype)


def _merge(ya, yb, wpa, wpb, proj, zcol0, tm, tn):
    n, d = ya.shape
    dm = wpa.shape[1]
    nj = dm // tn
    return pl.pallas_call(
        _merge_kernel,
        grid=(n // tm, nj),
        in_specs=[
            pl.BlockSpec((tm, d), lambda i, j: (i, 0)),
            pl.BlockSpec((tm, d), lambda i, j: (i, 0)),
            pl.BlockSpec((d, tn), lambda i, j: (0, j)),
            pl.BlockSpec((d, tn), lambda i, j: (0, j)),
            pl.BlockSpec((tm, tn), lambda i, j: (i, zcol0 + j)),
            pl.BlockSpec((tm, tn), lambda i, j: (i, zcol0 + nj + j)),
        ],
        out_specs=pl.BlockSpec((tm, tn), lambda i, j: (i, j)),
        out_shape=jax.ShapeDtypeStruct((n, dm), BF16),
        compiler_params=_params(("parallel", "arbitrary")),
        name="merge",
    )(ya, yb, wpa, wpb, proj, proj)


def _outproj_kernel(m_ref, w_ref, x_ref, g_ref, h_ref, xnt_ref):
    h = x_ref[...] + jnp.dot(m_ref[...], w_ref[...], preferred_element_type=F32)
    h_ref[...] = h
    ms = jnp.mean(h * h, axis=-1, keepdims=True)
    xn = h * lax.rsqrt(ms + EPS) * g_ref[...]
    xnt_ref[...] = xn.T.astype(xnt_ref.dtype)


def _outproj(mixed, w_out, x, g, tm):
    n, d = x.shape
    return pl.pallas_call(
        _outproj_kernel,
        grid=(n // tm,),
        in_specs=[
            pl.BlockSpec((tm, d), lambda i: (i, 0)),
            pl.BlockSpec((d, d), lambda i: (0, 0)),
            pl.BlockSpec((tm, d), lambda i: (i, 0)),
            pl.BlockSpec((1, d), lambda i: (0, 0)),
        ],
        out_specs=[pl.BlockSpec((tm, d), lambda i: (i, 0)), pl.BlockSpec((d, tm), lambda i: (0, i))],
        out_shape=[jax.ShapeDtypeStruct((n, d), F32), jax.ShapeDtypeStruct((d, n), BF16)],
        compiler_params=_params(("parallel",)),
        name="outproj",
    )(mixed, w_out, x, g)


def _top_ranks(s, k):
    rank = jnp.full(s.shape, float(N_KEYS), F32)
    vals = []
    for j in range(k):
        m = jnp.max(s, axis=0, keepdims=True)
        hit = s == m
        rank = jnp.where(hit, float(j), rank)
        s = jnp.where(hit, -jnp.inf, s)
        vals.append(m)
    return rank, vals


def _retrieve_kernel(xnt_ref, wqt_ref, keys_ref, r2_ref, e2_ref, m_ref, w_ref):
    heads = keys_ref.shape[0]
    dq = keys_ref.shape[3]
    k = PEER_TOPK
    tm = xnt_ref.shape[1]
    qt = jnp.dot(wqt_ref[...], xnt_ref[...], preferred_element_type=F32).astype(BF16)
    for h in range(heads):
        s1_all = jnp.dot(keys_ref[h, 0], qt[(2 * h) * dq:(2 * h + 1) * dq], preferred_element_type=F32)
        s2_all = jnp.dot(keys_ref[h, 1], qt[(2 * h + 1) * dq:(2 * h + 2) * dq], preferred_element_type=F32)
        for lo in range(0, tm, LANES):
            cols = slice(lo, lo + LANES)
            s1, s2 = s1_all[:, cols], s2_all[:, cols]
            r1, t1 = _top_ranks(s1, k)
            r2, t2 = _top_ranks(s2, k)
            t2s = jnp.concatenate(t2, axis=0)
            pieces = [t1[i] + t2s[:k // (i + 1)] for i in range(k)]
            rows = sum(p.shape[0] for p in pieces)
            pieces.append(jnp.full((-rows % 8, LANES), -jnp.inf, F32))
            work = jnp.concatenate(pieces, axis=0)
            tau = None
            for _ in range(k):
                tau = jnp.max(work, axis=0, keepdims=True)
                work = jnp.where(work == tau, -jnp.inf, work)
            e2s = jnp.exp(t2s - t2[0])
            z = jnp.zeros_like(tau)
            m = jnp.zeros_like(s1)
            for i in range(k):
                sel = (t1[i] + t2s) >= tau
                cnt = jnp.sum(sel.astype(F32), axis=0, keepdims=True)
                z = z + jnp.exp(t1[i] - t1[0]) * jnp.sum(jnp.where(sel, e2s, 0.0), axis=0, keepdims=True)
                m = jnp.where(r1 == float(i), cnt, m)
            r2_ref[h, :, cols] = r2.astype(r2_ref.dtype)
            e2_ref[h, :, cols] = jnp.exp(s2 - t2[0]).astype(e2_ref.dtype)
            m_ref[h, :, cols] = m
            w_ref[h, :, cols] = (0.5 * jnp.exp(s1 - t1[0])) / z


def _retrieve(xnt, wqt, keys, tm):
    d, n = xnt.shape
    heads = keys.shape[0]
    assert tm % LANES == 0
    big = lambda: pl.BlockSpec((heads, N_KEYS, tm), lambda i: (0, 0, i))
    shape = lambda dt: jax.ShapeDtypeStruct((heads, N_KEYS, n), dt)
    return pl.pallas_call(
        _retrieve_kernel,
        grid=(n // tm,),
        in_specs=[
            pl.BlockSpec((d, tm), lambda i: (0, i)),
            pl.BlockSpec(wqt.shape, lambda i: (0, 0)),
            pl.BlockSpec(keys.shape, lambda i: (0, 0, 0, 0)),
        ],
        out_specs=[big(), big(), big(), big()],
        out_shape=[shape(BF16), shape(BF16), shape(F32), shape(F32)],
        compiler_params=_params(("parallel",)),
        name="peer_retrieve",
    )(xnt, wqt, keys)


def _experts_kernel(n_blk, xnt_ref, u_ref, v_ref, r2_ref, e2_ref, m_ref, w_ref, h_ref, g_ref, o_ref,
                    act0_ref, act1_ref, acc_ref):
    s = pl.program_id(0)
    heads = r2_ref.shape[0]
    ec = u_ref.shape[0]
    e_prev = jnp.maximum(s - 1, 0) % n_blk

    @pl.when(s == 0)
    def _():
        act1_ref[...] = jnp.zeros_like(act1_ref)

    @pl.when(e_prev == 0)
    def _():
        acc_ref[...] = jnp.zeros_like(acc_ref)

    def step(src_ref, dst_ref):
        c0 = e_prev * (ec // N_KEYS)
        mrows = [[m_ref[h, pl.ds(c0 + cc, 1), :].astype(BF16) for h in range(heads)]
                 for cc in range(ec // N_KEYS)]
        wrows = [[w_ref[h, pl.ds(c0 + cc, 1), :].astype(BF16) for h in range(heads)]
                 for cc in range(ec // N_KEYS)]
        for k0 in range(0, ec, EXPERT_GROUP):
            grp = slice(k0, k0 + EXPERT_GROUP)
            dst_ref[grp, :] = jnp.dot(u_ref[grp, :], xnt_ref[...], preferred_element_type=F32)
            ws = []
            for r0 in range(k0, k0 + EXPERT_GROUP, N_KEYS):
                gate = None
                for h in range(heads):
                    gh = jnp.where(r2_ref[h] < mrows[r0 // N_KEYS][h], e2_ref[h], 0) * wrows[r0 // N_KEYS][h]
                    gate = gh if gate is None else gate + gh
                a = src_ref[r0:r0 + N_KEYS, :]
                ge = a * (1.0 + lax.erf(a * (1.0 / math.sqrt(2.0))))
                ws.append(ge.astype(BF16) * gate)
            wt = jnp.concatenate(ws, axis=0).T
            acc_ref[...] += jnp.dot(wt, v_ref[grp, :], preferred_element_type=F32)

    @pl.when(s % 2 == 0)
    def _():
        step(act1_ref, act0_ref)

    @pl.when(s % 2 == 1)
    def _():
        step(act0_ref, act1_ref)

    @pl.when(jnp.logical_and(s > 0, e_prev == n_blk - 1))
    def _():
        hh = h_ref[...] + acc_ref[...]
        ms = jnp.mean(hh * hh, axis=-1, keepdims=True)
        o_ref[...] = hh * lax.rsqrt(ms + EPS) * g_ref[...]


def _experts(xnt, u, v, r2, e2, m, w, h2, g, tm, ec):
    d, n = xnt.shape
    n_exp = u.shape[0]
    heads = r2.shape[0]
    n_blk = n_exp // ec
    assert ec % EXPERT_GROUP == 0 and EXPERT_GROUP % N_KEYS == 0
    steps = (n // tm) * n_blk
    cur = lambda s: jnp.minimum(s, steps - 1)
    prev = lambda s: jnp.maximum(s - 1, 0)
    tile_spec = lambda shape, imap: pl.BlockSpec(shape, imap, pipeline_mode=pl.Buffered(1))
    big = lambda: tile_spec((heads, N_KEYS, tm), lambda s: (0, 0, prev(s) // n_blk))
    return pl.pallas_call(
        functools.partial(_experts_kernel, n_blk),
        grid=(steps + 1,),
        in_specs=[
            pl.BlockSpec((d, tm), lambda s: (0, cur(s) // n_blk)),
            pl.BlockSpec((ec, d), lambda s: (cur(s) % n_blk, 0)),
            pl.BlockSpec((ec, d), lambda s: (prev(s) % n_blk, 0)),
            big(), big(), big(), big(),
            tile_spec((tm, d), lambda s: (prev(s) // n_blk, 0)),
            pl.BlockSpec((1, d), lambda s: (0, 0)),
        ],
        out_specs=pl.BlockSpec((tm, d), lambda s: (prev(s) // n_blk, 0)),
        out_shape=jax.ShapeDtypeStruct((n, d), F32),
        scratch_shapes=[pltpu.VMEM((ec, tm), F32), pltpu.VMEM((ec, tm), F32), pltpu.VMEM((tm, d), F32)],
        compiler_params=_params(("arbitrary",)),
        name="peer_experts",
    )(xnt, u, v, r2, e2, m, w, h2, g)


def _tile(n, want):
    t = min(n, want)
    assert n % t == 0, (n, t)
    return t


def kernel(x, meta, ln1_g, w_in, conv_w, conv_b, rg_wa, rg_ba, rg_wx, rg_bx, rg_lambda, hg_lb_logits,
           hg_norm_g, w_pa, w_pb, w_out, ln2_g, peer_wq, peer_keys, peer_u, peer_v, final_g):
    batch, seq, d = x.shape
    n = batch * seq
    n_meta = meta.shape[0]
    depth = w_in.shape[0]
    assert depth == 1 and seq % TT == 0 and n_meta <= TT
    rg_width = conv_w.shape[2]
    hg_width = hg_norm_g.shape[1]
    assert rg_width == hg_width == d and rg_wa.shape[2] == HEAD
    n_pad = TT - n_meta
    row = lambda a: a.reshape(1, -1)

    xf = x.reshape(n, d)
    meta_tile = jnp.concatenate([jnp.zeros((n_pad, d), x.dtype), meta.astype(x.dtype)], axis=0)

    w_in_b = w_in[0].astype(BF16)
    tm = _tile(n, 1024)
    proj = _norm_matmul(xf, row(ln1_g[0]), w_in_b, tm, 1024)
    proj_meta = _norm_matmul(meta_tile, row(ln1_g[0]), w_in_b, TT, 1024)

    y_a = _rglru(proj, proj_meta, batch, seq, n_pad, conv_w[0], row(conv_b[0]), rg_wa[0].astype(BF16),
                 row(rg_ba[0]), rg_wx[0].astype(BF16), row(rg_bx[0]), row(rg_lambda[0]))
    y_b = _hgrn(proj, proj_meta, batch, seq, n_pad, 2, hg_lb_logits, row(hg_norm_g[0]))

    tn = 1024
    mixed = _merge(y_a, y_b, w_pa[0].astype(BF16), w_pb[0].astype(BF16), proj, 6 * (d // tn), tm, tn)
    h2, xn2t = _outproj(mixed, w_out[0].astype(BF16), xf, row(ln2_g[0]), _tile(n, 512))

    tp = _tile(n, 512)
    r2, e2, m, w = _retrieve(xn2t, peer_wq[0].T.astype(BF16), peer_keys[0].astype(BF16), tp)
    out = _experts(xn2t, peer_u[0].astype(BF16), peer_v[0].astype(BF16), r2, e2, m, w, h2,
                   row(final_g), tp, 1024)
    return out.reshape(batch, seq, d)
```

```python
import functools
import math

import jax
import jax.numpy as jnp
from jax import lax
from jax.experimental import pallas as pl
from jax.experimental.pallas import tpu as pltpu

F32 = jnp.float32
BF16 = jnp.bfloat16

EPS = 1e-6
RG_C = 8.0
CONV_WIDTH = 4
HEAD = 128
N_KEYS = 128
PEER_TOPK = 16
TT = 128
RB = 32
LANES = 128
EXPERT_GROUP = 1024
VMEM_LIMIT = 56 * 1024 * 1024

NT_DIMS = (((1,), (1,)), ((), ()))
TN_DIMS = (((0,), (0,)), ((), ()))


def _params(sem):
    return pltpu.CompilerParams(dimension_semantics=sem, vmem_limit_bytes=VMEM_LIMIT)


def _sigmoid(x):
    return jax.nn.sigmoid(x)


def _gelu(x):
    return 0.5 * x * (1.0 + lax.erf(x * (1.0 / math.sqrt(2.0))))


def _norm_mm_kernel(x_ref, g_ref, w_ref, o_ref, xn_ref):
    @pl.when(pl.program_id(1) == 0)
    def _():
        xf = x_ref[...]
        ms = jnp.mean(xf * xf, axis=-1, keepdims=True)
        xn_ref[...] = (xf * lax.rsqrt(ms + EPS) * g_ref[...]).astype(BF16)

    o_ref[...] = jnp.dot(xn_ref[...], w_ref[...], preferred_element_type=F32).astype(o_ref.dtype)


def _norm_matmul(x, g, w, tm, tn):
    m, d = x.shape
    n = w.shape[1]
    return pl.pallas_call(
        _norm_mm_kernel,
        grid=(m // tm, n // tn),
        in_specs=[
            pl.BlockSpec((tm, d), lambda i, j: (i, 0)),
            pl.BlockSpec((1, d), lambda i, j: (0, 0)),
            pl.BlockSpec((d, tn), lambda i, j: (0, j)),
        ],
        out_specs=pl.BlockSpec((tm, tn), lambda i, j: (i, j)),
        out_shape=jax.ShapeDtypeStruct((m, n), BF16),
        scratch_shapes=[pltpu.VMEM((tm, d), BF16)],
        compiler_params=_params(("parallel", "arbitrary")),
        name="norm_matmul",
    )(x, g, w)


def _rglru_kernel(n_pad, xa_ref, ya_ref, xam_ref, yam_ref, cw_ref, cb_ref, wa_ref, ba_ref, wx_ref, bx_ref,
                  lam_ref, o_ref, xbuf_ref, hc_ref):
    t = pl.program_id(1)
    width = xa_ref.shape[1]
    is_meta = t == 0

    @pl.when(is_meta)
    def _():
        xbuf_ref[0:8, :] = jnp.zeros((8, width), F32)
        hc_ref[...] = jnp.zeros_like(hc_ref)

    xbuf_ref[8:8 + TT, :] = jnp.where(is_meta, xam_ref[...], xa_ref[...]).astype(F32)
    row = lax.broadcasted_iota(jnp.int32, (TT, HEAD), 0)
    live = jnp.logical_or(t > 0, row >= n_pad)
    r8 = row & 7

    for h in range(width // HEAD):
        sl = slice(h * HEAD, (h + 1) * HEAD)
        xc = cb_ref[:, sl] + cw_ref[0:1, sl] * xbuf_ref[5:5 + TT, sl]
        for k in range(1, CONV_WIDTH):
            xc = xc + cw_ref[k:k + 1, sl] * xbuf_ref[5 + k:5 + k + TT, sl]
        xcb = xc.astype(BF16)
        r = _sigmoid(jnp.dot(xcb, wa_ref[h], preferred_element_type=F32) + ba_ref[:, sl])
        i = _sigmoid(jnp.dot(xcb, wx_ref[h], preferred_element_type=F32) + bx_ref[:, sl])
        sp = jax.nn.softplus(-lam_ref[:, sl])
        log_a = (-RG_C) * r * sp
        a = jnp.exp(log_a)
        th = jnp.tanh(log_a)
        u = jnp.sqrt(-2.0 * th / (1.0 - th)) * (i * xc)
        u = jnp.where(live, u, 0.0)
        for k in (1, 2, 4):
            a_sh = pltpu.roll(a, k, 0)
            u_sh = pltpu.roll(u, k, 0)
            m = r8 >= k
            u = jnp.where(m, a * u_sh + u, u)
            a = jnp.where(m, a * a_sh, a)
        carry = hc_ref[:, sl]
        outs = []
        for g in range(TT // 8):
            hg = a[g * 8:(g + 1) * 8] * carry + u[g * 8:(g + 1) * 8]
            outs.append(hg)
            carry = hg[7:8]
        hc_ref[:, sl] = carry
        hs = jnp.concatenate(outs, axis=0)
        ya = jnp.where(is_meta, yam_ref[:, sl], ya_ref[:, sl]).astype(F32)
        o_ref[:, sl] = (_gelu(ya) * hs).astype(o_ref.dtype)

    xbuf_ref[0:8, :] = xbuf_ref[TT:TT + 8, :]


def _rglru(proj, proj_meta, batch, seq, n_pad, conv_w, conv_b, wa, ba, wx, bx, lam):
    n = batch * seq
    width = conv_w.shape[1]
    tiles = seq // TT
    in_row = lambda b, t: b * tiles + jnp.maximum(t - 1, 0)

    vec = lambda: pl.BlockSpec((1, width), lambda b, t: (0, 0))
    gate_w = lambda: pl.BlockSpec((width // HEAD, HEAD, HEAD), lambda b, t: (0, 0, 0))
    return pl.pallas_call(
        functools.partial(_rglru_kernel, n_pad),
        grid=(batch, tiles + 1),
        in_specs=[
            pl.BlockSpec((TT, width), lambda b, t: (in_row(b, t), 0)),
            pl.BlockSpec((TT, width), lambda b, t: (in_row(b, t), 1)),
            pl.BlockSpec((TT, width), lambda b, t: (0, 0)),
            pl.BlockSpec((TT, width), lambda b, t: (0, 1)),
            pl.BlockSpec((CONV_WIDTH, width), lambda b, t: (0, 0)),
            vec(), gate_w(), vec(), gate_w(), vec(), vec(),
        ],
        out_specs=pl.BlockSpec((TT, width), lambda b, t: (in_row(b, t), 0)),
        out_shape=jax.ShapeDtypeStruct((n, width), BF16),
        scratch_shapes=[pltpu.VMEM((TT + 8, width), F32), pltpu.VMEM((1, width), F32)],
        compiler_params=_params(("parallel", "arbitrary")),
        name="rglru",
    )(proj, proj, proj_meta, proj_meta, conv_w, conv_b, wa, ba, wx, bx, lam)


def _hgrn_kernel(n_pad, q_ref, f_ref, v_ref, g_ref, qm_ref, fm_ref, vm_ref, gm_ref, lbl_ref, ng_ref, o_ref,
                 st_ref):
    t = pl.program_id(1)
    width = q_ref.shape[1]
    is_meta = t == 0

    @pl.when(is_meta)
    def _():
        st_ref[...] = jnp.zeros_like(st_ref)

    pick = lambda meta_ref, ref: jnp.where(is_meta, meta_ref[...], ref[...])
    q_in, f_in, v_in, g_in = pick(qm_ref, q_ref), pick(fm_ref, f_ref), pick(vm_ref, v_ref), pick(gm_ref, g_ref)

    heads = [slice(h * HEAD, (h + 1) * HEAD) for h in range(width // HEAD)]
    blocks = [(r0, r0 + RB, r0 + RB // 2 - 1) for r0 in range(0, TT, RB)]

    row = lax.broadcasted_iota(jnp.int32, (TT, width), 0)
    live = jnp.logical_or(t > 0, row >= n_pad)
    lg = lbl_ref[...]
    e = jnp.exp(lg - jnp.max(lg, axis=0, keepdims=True))
    lb = e[0:1] / jnp.sum(e, axis=0, keepdims=True)
    f = lb + (1.0 - lb) * _sigmoid(f_in.astype(F32))
    kk = 1.0 - f
    bcum = jnp.where(live, jnp.log(f), 0.0)
    k = 1
    while k < TT:
        bcum = bcum + jnp.where(row >= k, pltpu.roll(bcum, k, 0), 0.0)
        k *= 2
    bend = bcum[TT - 1:TT]
    q = q_in.astype(F32)
    qs = q * _sigmoid(q)
    qhat = (qs * jnp.exp(bcum)).astype(BF16)
    khat = (kk * jnp.exp(bend - bcum)).astype(BF16)
    dec = jnp.exp(bend)
    qts, kts = [], []
    for r0, r1, mid in blocks:
        bref = bcum[mid:mid + 1]
        qts.append((qs[r0:r1] * jnp.exp(bcum[r0:r1] - bref)).astype(BF16))
        kt = kk * jnp.exp(bref - bcum)
        kts.append((kt if r1 == TT else jnp.where(row < r1, kt, 0.0)).astype(BF16))

    tri = lax.broadcasted_iota(jnp.int32, (TT, TT), 0) >= lax.broadcasted_iota(jnp.int32, (TT, TT), 1)
    o_inter = [lax.dot_general(qhat[:, sl], st_ref[h].astype(BF16), NT_DIMS, preferred_element_type=F32)
               for h, sl in enumerate(heads)]
    scores = [[lax.dot_general(qt[:, sl], kt[:, sl], NT_DIMS, preferred_element_type=F32)
               for qt, kt in zip(qts, kts)] for sl in heads]
    outs = []
    for h, sl in enumerate(heads):
        p = jnp.where(tri, jnp.concatenate(scores[h], axis=0), 0.0).astype(BF16)
        outs.append(jnp.dot(p, v_in[:, sl], preferred_element_type=F32) + o_inter[h])
    for h, sl in enumerate(heads):
        st_ref[h] = st_ref[h] * dec[:, sl] + lax.dot_general(v_in[:, sl], khat[:, sl], TN_DIMS,
                                                             preferred_element_type=F32)
    for h, sl in enumerate(heads):
        o = outs[h]
        o = o * lax.rsqrt(jnp.mean(o * o, axis=-1, keepdims=True) + EPS)
        og = g_in[:, sl].astype(F32)
        o_ref[:, sl] = (o * ng_ref[:, sl] * (og * _sigmoid(og))).astype(o_ref.dtype)


def _hgrn(proj, proj_meta, batch, seq, n_pad, col0, lb_logits, norm_g):
    n = batch * seq
    width = norm_g.shape[1]
    tiles = seq // TT
    in_row = lambda b, t: b * tiles + jnp.maximum(t - 1, 0)

    col = lambda c: pl.BlockSpec((TT, width), lambda b, t: (in_row(b, t), col0 + c))
    meta_col = lambda c: pl.BlockSpec((TT, width), lambda b, t: (0, col0 + c))
    return pl.pallas_call(
        functools.partial(_hgrn_kernel, n_pad),
        grid=(batch, tiles + 1),
        in_specs=[
            col(0), col(1), col(2), col(3),
            meta_col(0), meta_col(1), meta_col(2), meta_col(3),
            pl.BlockSpec(lb_logits.shape, lambda b, t: (0, 0)),
            pl.BlockSpec((1, width), lambda b, t: (0, 0)),
        ],
        out_specs=pl.BlockSpec((TT, width), lambda b, t: (in_row(b, t), 0)),
        out_shape=jax.ShapeDtypeStruct((n, width), BF16),
        scratch_shapes=[pltpu.VMEM((width // HEAD, HEAD, HEAD), F32)],
        compiler_params=_params(("parallel", "arbitrary")),
        name="hgrn2",
    )(proj, proj, proj, proj, proj_meta, proj_meta, proj_meta, proj_meta, lb_logits, norm_g)


def _merge_kernel(ya_ref, yb_ref, wpa_ref, wpb_ref, za_ref, zb_ref, o_ref):
    ta = jnp.dot(ya_ref[...], wpa_ref[...], preferred_element_type=F32)
    tb = jnp.dot(yb_ref[...], wpb_ref[...], preferred_element_type=F32)
    mixed = _sigmoid(za_ref[...].astype(F32)) * ta + _sigmoid(zb_ref[...].astype(F32)) * tb
    o_ref[...] = mixed.astype(o_ref.dtype)


def _merge(ya, yb, wpa, wpb, proj, zcol0, tm, tn):
    n, d = ya.shape
    dm = wpa.shape[1]
    nj = dm // tn
    return pl.pallas_call(
        _merge_kernel,
        grid=(n // tm, nj),
        in_specs=[
            pl.BlockSpec((tm, d), lambda i, j: (i, 0)),
            pl.BlockSpec((tm, d), lambda i, j: (i, 0)),
            pl.BlockSpec((d, tn), lambda i, j: (0, j)),
            pl.BlockSpec((d, tn), lambda i, j: (0, j)),
            pl.BlockSpec((tm, tn), lambda i, j: (i, zcol0 + j)),
            pl.BlockSpec((tm, tn), lambda i, j: (i, zcol0 + nj + j)),
        ],
        out_specs=pl.BlockSpec((tm, tn), lambda i, j: (i, j)),
        out_shape=jax.ShapeDtypeStruct((n, dm), BF16),
        compiler_params=_params(("parallel", "arbitrary")),
        name="merge",
    )(ya, yb, wpa, wpb, proj, proj)


def _outproj_kernel(m_ref, w_ref, x_ref, g_ref, h_ref, xnt_ref):
    h = x_ref[...] + jnp.dot(m_ref[...], w_ref[...], preferred_element_type=F32)
    h_ref[...] = h
    ms = jnp.mean(h * h, axis=-1, keepdims=True)
    xn = h * lax.rsqrt(ms + EPS) * g_ref[...]
    xnt_ref[...] = xn.T.astype(xnt_ref.dtype)


def _outproj(mixed, w_out, x, g, tm):
    n, d = x.shape
    return pl.pallas_call(
        _outproj_kernel,
        grid=(n // tm,),
        in_specs=[
            pl.BlockSpec((tm, d), lambda i: (i, 0)),
            pl.BlockSpec((d, d), lambda i: (0, 0)),
            pl.BlockSpec((tm, d), lambda i: (i, 0)),
            pl.BlockSpec((1, d), lambda i: (0, 0)),
        ],
        out_specs=[pl.BlockSpec((tm, d), lambda i: (i, 0)), pl.BlockSpec((d, tm), lambda i: (0, i))],
        out_shape=[jax.ShapeDtypeStruct((n, d), F32), jax.ShapeDtypeStruct((d, n), BF16)],
        compiler_params=_params(("parallel",)),
        name="outproj",
    )(mixed, w_out, x, g)


def _top_ranks(s, k):
    rank = jnp.full(s.shape, float(N_KEYS), F32)
    vals = []
    for j in range(k):
        m = jnp.max(s, axis=0, keepdims=True)
        hit = s == m
        rank = jnp.where(hit, float(j), rank)
        s = jnp.where(hit, -jnp.inf, s)
        vals.append(m)
    return rank, vals


def _retrieve_kernel(xnt_ref, wqt_ref, keys_ref, r2_ref, e2_ref, m_ref, w_ref):
    heads = keys_ref.shape[0]
    dq = keys_ref.shape[3]
    k = PEER_TOPK
    tm = xnt_ref.shape[1]
    qt = jnp.dot(wqt_ref[...], xnt_ref[...], preferred_element_type=F32).astype(BF16)
    for h in range(heads):
        s1_all = jnp.dot(keys_ref[h, 0], qt[(2 * h) * dq:(2 * h + 1) * dq], preferred_element_type=F32)
        s2_all = jnp.dot(keys_ref[h, 1], qt[(2 * h + 1) * dq:(2 * h + 2) * dq], preferred_element_type=F32)
        for lo in range(0, tm, LANES):
            cols = slice(lo, lo + LANES)
            s1, s2 = s1_all[:, cols], s2_all[:, cols]
            r1, t1 = _top_ranks(s1, k)
            r2, t2 = _top_ranks(s2, k)
            t2s = jnp.concatenate(t2, axis=0)
            pieces = [t1[i] + t2s[:k // (i + 1)] for i in range(k)]
            rows = sum(p.shape[0] for p in pieces)
            pieces.append(jnp.full((-rows % 8, LANES), -jnp.inf, F32))
            work = jnp.concatenate(pieces, axis=0)
            tau = None
            for _ in range(k):
                tau = jnp.max(work, axis=0, keepdims=True)
                work = jnp.where(work == tau, -jnp.inf, work)
            e2s = jnp.exp(t2s - t2[0])
            z = jnp.zeros_like(tau)
            m = jnp.zeros_like(s1)
            for i in range(k):
                sel = (t1[i] + t2s) >= tau
                cnt = jnp.sum(sel.astype(F32), axis=0, keepdims=True)
                z = z + jnp.exp(t1[i] - t1[0]) * jnp.sum(jnp.where(sel, e2s, 0.0), axis=0, keepdims=True)
                m = jnp.where(r1 == float(i), cnt, m)
            r2_ref[h, :, cols] = r2.astype(r2_ref.dtype)
            e2_ref[h, :, cols] = jnp.exp(s2 - t2[0]).astype(e2_ref.dtype)
            m_ref[h, :, cols] = m
            w_ref[h, :, cols] = (0.5 * jnp.exp(s1 - t1[0])) / z


def _retrieve(xnt, wqt, keys, tm):
    d, n = xnt.shape
    heads = keys.shape[0]
    assert tm % LANES == 0
    big = lambda: pl.BlockSpec((heads, N_KEYS, tm), lambda i: (0, 0, i))
    shape = lambda dt: jax.ShapeDtypeStruct((heads, N_KEYS, n), dt)
    return pl.pallas_call(
        _retrieve_kernel,
        grid=(n // tm,),
        in_specs=[
            pl.BlockSpec((d, tm), lambda i: (0, i)),
            pl.BlockSpec(wqt.shape, lambda i: (0, 0)),
            pl.BlockSpec(keys.shape, lambda i: (0, 0, 0, 0)),
        ],
        out_specs=[big(), big(), big(), big()],
        out_shape=[shape(BF16), shape(BF16), shape(F32), shape(F32)],
        compiler_params=_params(("parallel",)),
        name="peer_retrieve",
    )(xnt, wqt, keys)


def _experts_kernel(n_blk, xnt_ref, u_ref, vt_ref, r2_ref, e2_ref, m_ref, w_ref, h_ref, g_ref, o_ref,
                    act0_ref, act1_ref, acc_ref):
    s = pl.program_id(0)
    heads = r2_ref.shape[0]
    ec = u_ref.shape[0]
    e_prev = jnp.maximum(s - 1, 0) % n_blk

    @pl.when(s == 0)
    def _():
        act1_ref[...] = jnp.zeros_like(act1_ref)

    @pl.when(e_prev == 0)
    def _():
        acc_ref[...] = jnp.zeros_like(acc_ref)

    def step(src_ref, dst_ref):
        c0 = e_prev * (ec // N_KEYS)
        mrows = [[m_ref[h, pl.ds(c0 + cc, 1), :].astype(BF16) for h in range(heads)]
                 for cc in range(ec // N_KEYS)]
        wrows = [[w_ref[h, pl.ds(c0 + cc, 1), :].astype(BF16) for h in range(heads)]
                 for cc in range(ec // N_KEYS)]
        for k0 in range(0, ec, EXPERT_GROUP):
            grp = slice(k0, k0 + EXPERT_GROUP)
            dst_ref[grp, :] = jnp.dot(u_ref[grp, :], xnt_ref[...], preferred_element_type=F32)
            ws = []
            for r0 in range(k0, k0 + EXPERT_GROUP, N_KEYS):
                gate = None
                for h in range(heads):
                    gh = jnp.where(r2_ref[h] < mrows[r0 // N_KEYS][h], e2_ref[h], 0) * wrows[r0 // N_KEYS][h]
                    gate = gh if gate is None else gate + gh
                a = src_ref[r0:r0 + N_KEYS, :]
                ge = a * (1.0 + lax.erf(a * (1.0 / math.sqrt(2.0))))
                ws.append(ge.astype(BF16) * gate)
            wg = jnp.concatenate(ws, axis=0)
            acc_ref[...] += jnp.dot(vt_ref[:, grp], wg, preferred_element_type=F32)

    @pl.when(s % 2 == 0)
    def _():
        step(act1_ref, act0_ref)

    @pl.when(s % 2 == 1)
    def _():
        step(act0_ref, act1_ref)

    @pl.when(jnp.logical_and(s > 0, e_prev == n_blk - 1))
    def _():
        hh = h_ref[...] + acc_ref[...].T
        ms = jnp.mean(hh * hh, axis=-1, keepdims=True)
        o_ref[...] = hh * lax.rsqrt(ms + EPS) * g_ref[...]


def _experts(xnt, u, vt, r2, e2, m, w, h2, g, tm, ec):
    d, n = xnt.shape
    n_exp = u.shape[0]
    heads = r2.shape[0]
    n_blk = n_exp // ec
    assert ec % EXPERT_GROUP == 0 and EXPERT_GROUP % N_KEYS == 0
    steps = (n // tm) * n_blk
    cur = lambda s: jnp.minimum(s, steps - 1)
    prev = lambda s: jnp.maximum(s - 1, 0)
    tile_spec = lambda shape, imap: pl.BlockSpec(shape, imap, pipeline_mode=pl.Buffered(1))
    big = lambda: tile_spec((heads, N_KEYS, tm), lambda s: (0, 0, prev(s) // n_blk))
    return pl.pallas_call(
        functools.partial(_experts_kernel, n_blk),
        grid=(steps + 1,),
        in_specs=[
            pl.BlockSpec((d, tm), lambda s: (0, cur(s) // n_blk)),
            pl.BlockSpec((ec, d), lambda s: (cur(s) % n_blk, 0)),
            pl.BlockSpec((d, ec), lambda s: (0, prev(s) % n_blk)),
            big(), big(), big(), big(),
            tile_spec((tm, d), lambda s: (prev(s) // n_blk, 0)),
            pl.BlockSpec((1, d), lambda s: (0, 0)),
        ],
        out_specs=pl.BlockSpec((tm, d), lambda s: (prev(s) // n_blk, 0)),
        out_shape=jax.ShapeDtypeStruct((n, d), F32),
        scratch_shapes=[pltpu.VMEM((ec, tm), F32), pltpu.VMEM((ec, tm), F32), pltpu.VMEM((d, tm), F32)],
        compiler_params=_params(("arbitrary",)),
        name="peer_experts",
    )(xnt, u, vt, r2, e2, m, w, h2, g)


def _tile(n, want):
    t = min(n, want)
    assert n % t == 0, (n, t)
    return t


def kernel(x, meta, ln1_g, w_in, conv_w, conv_b, rg_wa, rg_ba, rg_wx, rg_bx, rg_lambda, hg_lb_logits,
           hg_norm_g, w_pa, w_pb, w_out, ln2_g, peer_wq, peer_keys, peer_u, peer_v, final_g):
    batch, seq, d = x.shape
    n = batch * seq
    n_meta = meta.shape[0]
    depth = w_in.shape[0]
    assert depth == 1 and seq % TT == 0 and n_meta <= TT
    rg_width = conv_w.shape[2]
    hg_width = hg_norm_g.shape[1]
    assert rg_width == hg_width == d and rg_wa.shape[2] == HEAD
    n_pad = TT - n_meta
    row = lambda a: a.reshape(1, -1)

    xf = x.reshape(n, d)
    meta_tile = jnp.concatenate([jnp.zeros((n_pad, d), x.dtype), meta.astype(x.dtype)], axis=0)

    w_in_b = w_in[0].astype(BF16)
    tm = _tile(n, 1024)
    proj = _norm_matmul(xf, row(ln1_g[0]), w_in_b, tm, 1024)
    proj_meta = _norm_matmul(meta_tile, row(ln1_g[0]), w_in_b, TT, 1024)

    y_a = _rglru(proj, proj_meta, batch, seq, n_pad, conv_w[0], row(conv_b[0]), rg_wa[0].astype(BF16),
                 row(rg_ba[0]), rg_wx[0].astype(BF16), row(rg_bx[0]), row(rg_lambda[0]))
    y_b = _hgrn(proj, proj_meta, batch, seq, n_pad, 2, hg_lb_logits, row(hg_norm_g[0]))

    tn = 1024
    mixed = _merge(y_a, y_b, w_pa[0].astype(BF16), w_pb[0].astype(BF16), proj, 6 * (d // tn), tm, tn)
    h2, xn2t = _outproj(mixed, w_out[0].astype(BF16), xf, row(ln2_g[0]), _tile(n, 512))

    tp = _tile(n, 512)
    r2, e2, m, w = _retrieve(xn2t, peer_wq[0].T.astype(BF16), peer_keys[0].astype(BF16), tp)
    out = _experts(xn2t, peer_u[0].astype(BF16), peer_v[0].T.astype(BF16), r2, e2, m, w, h2,
                   row(final_g), tp, 1024)
    return out.reshape(batch, seq, d)
```

```python
import functools
import math

import jax
import jax.numpy as jnp
from jax import lax
from jax.experimental import pallas as pl
from jax.experimental.pallas import tpu as pltpu

F32 = jnp.float32
BF16 = jnp.bfloat16

EPS = 1e-6
RG_C = 8.0
CONV_WIDTH = 4
HEAD = 128
N_KEYS = 128
PEER_TOPK = 16
TT = 128
RB = 32
LANES = 128
EXPERT_GROUP = 1024
VMEM_LIMIT = 56 * 1024 * 1024

NT_DIMS = (((1,), (1,)), ((), ()))
TN_DIMS = (((0,), (0,)), ((), ()))


def _params(sem):
    return pltpu.CompilerParams(dimension_semantics=sem, vmem_limit_bytes=VMEM_LIMIT)


def _sigmoid(x):
    return jax.nn.sigmoid(x)


def _gelu(x):
    return 0.5 * x * (1.0 + lax.erf(x * (1.0 / math.sqrt(2.0))))


def _norm_mm_kernel(x_ref, g_ref, w_ref, o_ref, xn_ref):
    @pl.when(pl.program_id(1) == 0)
    def _():
        xf = x_ref[...]
        ms = jnp.mean(xf * xf, axis=-1, keepdims=True)
        xn_ref[...] = (xf * lax.rsqrt(ms + EPS) * g_ref[...]).astype(BF16)

    o_ref[...] = jnp.dot(xn_ref[...], w_ref[...], preferred_element_type=F32).astype(o_ref.dtype)


def _norm_matmul(x, g, w, tm, tn):
    m, d = x.shape
    n = w.shape[1]
    return pl.pallas_call(
        _norm_mm_kernel,
        grid=(m // tm, n // tn),
        in_specs=[
            pl.BlockSpec((tm, d), lambda i, j: (i, 0)),
            pl.BlockSpec((1, d), lambda i, j: (0, 0)),
            pl.BlockSpec((d, tn), lambda i, j: (0, j)),
        ],
        out_specs=pl.BlockSpec((tm, tn), lambda i, j: (i, j)),
        out_shape=jax.ShapeDtypeStruct((m, n), BF16),
        scratch_shapes=[pltpu.VMEM((tm, d), BF16)],
        compiler_params=_params(("parallel", "arbitrary")),
        name="norm_matmul",
    )(x, g, w)


def _rglru_kernel(n_pad, xa_ref, ya_ref, xam_ref, yam_ref, cw_ref, cb_ref, wa_ref, ba_ref, wx_ref, bx_ref,
                  lam_ref, o_ref, xbuf_ref, hc_ref):
    t = pl.program_id(1)
    width = xa_ref.shape[1]
    is_meta = t == 0

    @pl.when(is_meta)
    def _():
        xbuf_ref[0:8, :] = jnp.zeros((8, width), F32)
        hc_ref[...] = jnp.zeros_like(hc_ref)

    xbuf_ref[8:8 + TT, :] = jnp.where(is_meta, xam_ref[...], xa_ref[...]).astype(F32)
    row = lax.broadcasted_iota(jnp.int32, (TT, HEAD), 0)
    live = jnp.logical_or(t > 0, row >= n_pad)
    r8 = row & 7

    for h in range(width // HEAD):
        sl = slice(h * HEAD, (h + 1) * HEAD)
        xc = cb_ref[:, sl] + cw_ref[0:1, sl] * xbuf_ref[5:5 + TT, sl]
        for k in range(1, CONV_WIDTH):
            xc = xc + cw_ref[k:k + 1, sl] * xbuf_ref[5 + k:5 + k + TT, sl]
        xcb = xc.astype(BF16)
        r = _sigmoid(jnp.dot(xcb, wa_ref[h], preferred_element_type=F32) + ba_ref[:, sl])
        i = _sigmoid(jnp.dot(xcb, wx_ref[h], preferred_element_type=F32) + bx_ref[:, sl])
        sp = jax.nn.softplus(-lam_ref[:, sl])
        log_a = (-RG_C) * r * sp
        a = jnp.exp(log_a)
        th = jnp.tanh(log_a)
        u = jnp.sqrt(-2.0 * th / (1.0 - th)) * (i * xc)
        u = jnp.where(live, u, 0.0)
        for k in (1, 2, 4):
            a_sh = pltpu.roll(a, k, 0)
            u_sh = pltpu.roll(u, k, 0)
            m = r8 >= k
            u = jnp.where(m, a * u_sh + u, u)
            a = jnp.where(m, a * a_sh, a)
        carry = hc_ref[:, sl]
        outs = []
        for g in range(TT // 8):
            hg = a[g * 8:(g + 1) * 8] * carry + u[g * 8:(g + 1) * 8]
            outs.append(hg)
            carry = hg[7:8]
        hc_ref[:, sl] = carry
        hs = jnp.concatenate(outs, axis=0)
        ya = jnp.where(is_meta, yam_ref[:, sl], ya_ref[:, sl]).astype(F32)
        o_ref[:, sl] = (_gelu(ya) * hs).astype(o_ref.dtype)

    xbuf_ref[0:8, :] = xbuf_ref[TT:TT + 8, :]


def _rglru(proj, proj_meta, batch, seq, n_pad, conv_w, conv_b, wa, ba, wx, bx, lam):
    n = batch * seq
    width = conv_w.shape[1]
    tiles = seq // TT
    in_row = lambda b, t: b * tiles + jnp.maximum(t - 1, 0)

    vec = lambda: pl.BlockSpec((1, width), lambda b, t: (0, 0))
    gate_w = lambda: pl.BlockSpec((width // HEAD, HEAD, HEAD), lambda b, t: (0, 0, 0))
    return pl.pallas_call(
        functools.partial(_rglru_kernel, n_pad),
        grid=(batch, tiles + 1),
        in_specs=[
            pl.BlockSpec((TT, width), lambda b, t: (in_row(b, t), 0)),
            pl.BlockSpec((TT, width), lambda b, t: (in_row(b, t), 1)),
            pl.BlockSpec((TT, width), lambda b, t: (0, 0)),
            pl.BlockSpec((TT, width), lambda b, t: (0, 1)),
            pl.BlockSpec((CONV_WIDTH, width), lambda b, t: (0, 0)),
            vec(), gate_w(), vec(), gate_w(), vec(), vec(),
        ],
        out_specs=pl.BlockSpec((TT, width), lambda b, t: (in_row(b, t), 0)),
        out_shape=jax.ShapeDtypeStruct((n, width), BF16),
        scratch_shapes=[pltpu.VMEM((TT + 8, width), F32), pltpu.VMEM((1, width), F32)],
        compiler_params=_params(("parallel", "arbitrary")),
        name="rglru",
    )(proj, proj, proj_meta, proj_meta, conv_w, conv_b, wa, ba, wx, bx, lam)


def _hgrn_kernel(n_pad, q_ref, f_ref, v_ref, g_ref, qm_ref, fm_ref, vm_ref, gm_ref, lbl_ref, ng_ref, o_ref,
                 st_ref):
    t = pl.program_id(1)
    width = q_ref.shape[1]
    is_meta = t == 0

    @pl.when(is_meta)
    def _():
        st_ref[...] = jnp.zeros_like(st_ref)

    pick = lambda meta_ref, ref: jnp.where(is_meta, meta_ref[...], ref[...])
    q_in, f_in, v_in, g_in = pick(qm_ref, q_ref), pick(fm_ref, f_ref), pick(vm_ref, v_ref), pick(gm_ref, g_ref)

    heads = [slice(h * HEAD, (h + 1) * HEAD) for h in range(width // HEAD)]
    blocks = [(r0, r0 + RB, r0 + RB // 2 - 1) for r0 in range(0, TT, RB)]

    row = lax.broadcasted_iota(jnp.int32, (TT, width), 0)
    live = jnp.logical_or(t > 0, row >= n_pad)
    lg = lbl_ref[...]
    e = jnp.exp(lg - jnp.max(lg, axis=0, keepdims=True))
    lb = e[0:1] / jnp.sum(e, axis=0, keepdims=True)
    f = lb + (1.0 - lb) * _sigmoid(f_in.astype(F32))
    kk = 1.0 - f
    bcum = jnp.where(live, jnp.log(f), 0.0)
    k = 1
    while k < TT:
        bcum = bcum + jnp.where(row >= k, pltpu.roll(bcum, k, 0), 0.0)
        k *= 2
    bend = bcum[TT - 1:TT]
    q = q_in.astype(F32)
    qs = q * _sigmoid(q)
    qhat = (qs * jnp.exp(bcum)).astype(BF16)
    khat = (kk * jnp.exp(bend - bcum)).astype(BF16)
    dec = jnp.exp(bend)
    qts, kts = [], []
    for r0, r1, mid in blocks:
        bref = bcum[mid:mid + 1]
        qts.append((qs[r0:r1] * jnp.exp(bcum[r0:r1] - bref)).astype(BF16))
        kt = kk * jnp.exp(bref - bcum)
        kts.append((kt if r1 == TT else jnp.where(row < r1, kt, 0.0)).astype(BF16))

    tri = lax.broadcasted_iota(jnp.int32, (TT, TT), 0) >= lax.broadcasted_iota(jnp.int32, (TT, TT), 1)
    o_inter = [lax.dot_general(qhat[:, sl], st_ref[h].astype(BF16), NT_DIMS, preferred_element_type=F32)
               for h, sl in enumerate(heads)]
    scores = [[lax.dot_general(qt[:, sl], kt[:, sl], NT_DIMS, preferred_element_type=F32)
               for qt, kt in zip(qts, kts)] for sl in heads]
    outs = []
    for h, sl in enumerate(heads):
        p = jnp.where(tri, jnp.concatenate(scores[h], axis=0), 0.0).astype(BF16)
        outs.append(jnp.dot(p, v_in[:, sl], preferred_element_type=F32) + o_inter[h])
    for h, sl in enumerate(heads):
        st_ref[h] = st_ref[h] * dec[:, sl] + lax.dot_general(v_in[:, sl], khat[:, sl], TN_DIMS,
                                                             preferred_element_type=F32)
    for h, sl in enumerate(heads):
        o = outs[h]
        o = o * lax.rsqrt(jnp.mean(o * o, axis=-1, keepdims=True) + EPS)
        og = g_in[:, sl].astype(F32)
        o_ref[:, sl] = (o * ng_ref[:, sl] * (og * _sigmoid(og))).astype(o_ref.dtype)


def _hgrn(proj, proj_meta, batch, seq, n_pad, col0, lb_logits, norm_g):
    n = batch * seq
    width = norm_g.shape[1]
    tiles = seq // TT
    in_row = lambda b, t: b * tiles + jnp.maximum(t - 1, 0)

    col = lambda c: pl.BlockSpec((TT, width), lambda b, t: (in_row(b, t), col0 + c))
    meta_col = lambda c: pl.BlockSpec((TT, width), lambda b, t: (0, col0 + c))
    return pl.pallas_call(
        functools.partial(_hgrn_kernel, n_pad),
        grid=(batch, tiles + 1),
        in_specs=[
            col(0), col(1), col(2), col(3),
            meta_col(0), meta_col(1), meta_col(2), meta_col(3),
            pl.BlockSpec(lb_logits.shape, lambda b, t: (0, 0)),
            pl.BlockSpec((1, width), lambda b, t: (0, 0)),
        ],
        out_specs=pl.BlockSpec((TT, width), lambda b, t: (in_row(b, t), 0)),
        out_shape=jax.ShapeDtypeStruct((n, width), BF16),
        scratch_shapes=[pltpu.VMEM((width // HEAD, HEAD, HEAD), F32)],
        compiler_params=_params(("parallel", "arbitrary")),
        name="hgrn2",
    )(proj, proj, proj, proj, proj_meta, proj_meta, proj_meta, proj_meta, lb_logits, norm_g)


def _merge_kernel(ya_ref, yb_ref, wpa_ref, wpb_ref, za_ref, zb_ref, o_ref):
    ta = jnp.dot(ya_ref[...], wpa_ref[...], preferred_element_type=F32)
    tb = jnp.dot(yb_ref[...], wpb_ref[...], preferred_element_type=F32)
    mixed = _sigmoid(za_ref[...].astype(F32)) * ta + _sigmoid(zb_ref[...].astype(F32)) * tb
    o_ref[...] = mixed.astype(o_ref.dtype)


def _merge(ya, yb, wpa, wpb, proj, zcol0, tm, tn):
    n, d = ya.shape
    dm = wpa.shape[1]
    nj = dm // tn
    return pl.pallas_call(
        _merge_kernel,
        grid=(n // tm, nj),
        in_specs=[
            pl.BlockSpec((tm, d), lambda i, j: (i, 0)),
            pl.BlockSpec((tm, d), lambda i, j: (i, 0)),
            pl.BlockSpec((d, tn), lambda i, j: (0, j)),
            pl.BlockSpec((d, tn), lambda i, j: (0, j)),
            pl.BlockSpec((tm, tn), lambda i, j: (i, zcol0 + j)),
            pl.BlockSpec((tm, tn), lambda i, j: (i, zcol0 + nj + j)),
        ],
        out_specs=pl.BlockSpec((tm, tn), lambda i, j: (i, j)),
        out_shape=jax.ShapeDtypeStruct((n, dm), BF16),
        compiler_params=_params(("parallel", "arbitrary")),
        name="merge",
    )(ya, yb, wpa, wpb, proj, proj)


def _outproj_kernel(m_ref, w_ref, x_ref, g_ref, h_ref, xnt_ref):
    h = x_ref[...] + jnp.dot(m_ref[...], w_ref[...], preferred_element_type=F32)
    h_ref[...] = h
    ms = jnp.mean(h * h, axis=-1, keepdims=True)
    xn = h * lax.rsqrt(ms + EPS) * g_ref[...]
    xnt_ref[...] = xn.T.astype(xnt_ref.dtype)


def _outproj(mixed, w_out, x, g, tm):
    n, d = x.shape
    return pl.pallas_call(
        _outproj_kernel,
        grid=(n // tm,),
        in_specs=[
            pl.BlockSpec((tm, d), lambda i: (i, 0)),
            pl.BlockSpec((d, d), lambda i: (0, 0)),
            pl.BlockSpec((tm, d), lambda i: (i, 0)),
            pl.BlockSpec((1, d), lambda i: (0, 0)),
        ],
        out_specs=[pl.BlockSpec((tm, d), lambda i: (i, 0)), pl.BlockSpec((d, tm), lambda i: (0, i))],
        out_shape=[jax.ShapeDtypeStruct((n, d), F32), jax.ShapeDtypeStruct((d, n), BF16)],
        compiler_params=_params(("parallel",)),
        name="outproj",
    )(mixed, w_out, x, g)


def _odd_even_merge_sort_pairs(n):
    pairs = []

    def merge(lo, cnt, r):
        step = r * 2
        if step < cnt:
            merge(lo, cnt, step)
            merge(lo + r, cnt, step)
            pairs.extend((i, i + r) for i in range(lo + r, lo + cnt - r, step))
        else:
            pairs.append((lo, lo + r))

    def sort(lo, cnt):
        if cnt > 1:
            sort(lo, cnt // 2)
            sort(lo + cnt // 2, cnt // 2)
            merge(lo, cnt, 1)

    sort(0, n)
    return pairs


def _order(v, i, j):
    v[i], v[j] = jnp.maximum(v[i], v[j]), jnp.minimum(v[i], v[j])


def _top_sorted(s):
    n = s.shape[0] // 8
    v = [s[8 * i:8 * (i + 1)] for i in range(n)]
    for i, j in _odd_even_merge_sort_pairs(n):
        _order(v, i, j)
    for shift in (4, 2, 1):
        v = [jnp.maximum(v[i], pltpu.roll(v[n - 1 - i], shift, 0)) for i in range(n)]
        d = n // 2
        while d:
            for i in range(n):
                if not i & d:
                    _order(v, i, i + d)
            d //= 2
    return v


def _prefix_len(t, test):
    b3 = test(t[7])
    b2 = test(jnp.where(b3, t[11], t[3]))
    b1 = test(jnp.where(b3, jnp.where(b2, t[13], t[9]), jnp.where(b2, t[5], t[1])))
    hi = jnp.where(b2, jnp.where(b1, t[14], t[12]), jnp.where(b1, t[10], t[8]))
    lo = jnp.where(b2, jnp.where(b1, t[6], t[4]), jnp.where(b1, t[2], t[0]))
    b0 = test(jnp.where(b3, hi, lo))
    p = jnp.where(b3, 8.0, 0.0) + jnp.where(b2, 4.0, 0.0) + jnp.where(b1, 2.0, 0.0) + jnp.where(b0, 1.0, 0.0)
    return p + jnp.where(test(t[15]), 1.0, 0.0)


def _retrieve_kernel(xnt_ref, wqt_ref, keys_ref, r2_ref, e2_ref, m_ref, w_ref):
    heads = keys_ref.shape[0]
    dq = keys_ref.shape[3]
    k = PEER_TOPK
    tm = xnt_ref.shape[1]
    for h in range(heads):
        qt = jnp.dot(wqt_ref[2 * h * dq:(2 * h + 2) * dq, :], xnt_ref[...], preferred_element_type=F32)
        qt = qt.astype(BF16)
        s1_all = jnp.dot(keys_ref[h, 0], qt[:dq], preferred_element_type=F32)
        s2_all = jnp.dot(keys_ref[h, 1], qt[dq:], preferred_element_type=F32)
        for lo in range(0, tm, LANES):
            cols = slice(lo, lo + LANES)
            s1, s2 = s1_all[:, cols], s2_all[:, cols]
            top1, top2 = _top_sorted(s1), _top_sorted(s2)
            t1 = [t[0:1] for t in top1]
            t2 = [t[0:1] for t in top2]
            t2s = jnp.concatenate(t2, axis=0)
            pieces = [t1[i] + t2s[:k // (i + 1)] for i in range(k)]
            rows = sum(p.shape[0] for p in pieces)
            pieces.append(jnp.full((-rows % 8, LANES), -jnp.inf, F32))
            work = jnp.concatenate(pieces, axis=0)
            tau = None
            for _ in range(k):
                tau = jnp.max(work, axis=0, keepdims=True)
                work = jnp.where(work == tau, -jnp.inf, work)
            e2s = jnp.exp(t2s - t2[0])
            z = jnp.zeros_like(tau)
            for i in range(k):
                sel = (t1[i] + t2s) >= tau
                z = z + jnp.exp(t1[i] - t1[0]) * jnp.sum(jnp.where(sel, e2s, 0.0), axis=0, keepdims=True)
            rows8 = [slice(8 * i, 8 * (i + 1)) for i in range(N_KEYS // 8)]
            r2 = jnp.concatenate([_prefix_len(top2, lambda t, x=s2[r]: t > x) for r in rows8], axis=0)
            m = jnp.concatenate([_prefix_len(top2, lambda t, x=s1[r]: (x + t) >= tau) for r in rows8], axis=0)
            r2_ref[h, :, cols] = r2.astype(r2_ref.dtype)
            e2_ref[h, :, cols] = jnp.exp(s2 - t2[0]).astype(e2_ref.dtype)
            m_ref[h, :, cols] = m
            w_ref[h, :, cols] = (0.5 * jnp.exp(s1 - t1[0])) / z


def _retrieve(xnt, wqt, keys, tm):
    d, n = xnt.shape
    heads = keys.shape[0]
    assert tm % LANES == 0
    assert PEER_TOPK == 16 and keys.shape[2] == N_KEYS == 8 * PEER_TOPK
    big = lambda: pl.BlockSpec((heads, N_KEYS, tm), lambda i: (0, 0, i))
    shape = lambda dt: jax.ShapeDtypeStruct((heads, N_KEYS, n), dt)
    return pl.pallas_call(
        _retrieve_kernel,
        grid=(n // tm,),
        in_specs=[
            pl.BlockSpec((d, tm), lambda i: (0, i)),
            pl.BlockSpec(wqt.shape, lambda i: (0, 0)),
            pl.BlockSpec(keys.shape, lambda i: (0, 0, 0, 0)),
        ],
        out_specs=[big(), big(), big(), big()],
        out_shape=[shape(BF16), shape(BF16), shape(F32), shape(F32)],
        compiler_params=_params(("parallel",)),
        name="peer_retrieve",
    )(xnt, wqt, keys)


def _experts_kernel(n_blk, xnt_ref, u_ref, vt_ref, r2_ref, e2_ref, m_ref, w_ref, h_ref, g_ref, o_ref,
                    act0_ref, act1_ref, acc_ref):
    s = pl.program_id(0)
    heads = r2_ref.shape[0]
    ec = u_ref.shape[0]
    e_prev = jnp.maximum(s - 1, 0) % n_blk

    @pl.when(s == 0)
    def _():
        act1_ref[...] = jnp.zeros_like(act1_ref)

    @pl.when(e_prev == 0)
    def _():
        acc_ref[...] = jnp.zeros_like(acc_ref)

    def step(src_ref, dst_ref):
        c0 = e_prev * (ec // N_KEYS)
        mrows = [[m_ref[h, pl.ds(c0 + cc, 1), :].astype(BF16) for h in range(heads)]
                 for cc in range(ec // N_KEYS)]
        wrows = [[w_ref[h, pl.ds(c0 + cc, 1), :].astype(BF16) for h in range(heads)]
                 for cc in range(ec // N_KEYS)]
        for k0 in range(0, ec, EXPERT_GROUP):
            grp = slice(k0, k0 + EXPERT_GROUP)
            dst_ref[grp, :] = jnp.dot(u_ref[grp, :], xnt_ref[...], preferred_element_type=F32)
            ws = []
            for r0 in range(k0, k0 + EXPERT_GROUP, N_KEYS):
                gate = None
                for h in range(heads):
                    gh = jnp.where(r2_ref[h] < mrows[r0 // N_KEYS][h], e2_ref[h], 0) * wrows[r0 // N_KEYS][h]
                    gate = gh if gate is None else gate + gh
                a = src_ref[r0:r0 + N_KEYS, :]
                ge = a * (1.0 + lax.erf(a * (1.0 / math.sqrt(2.0))))
                ws.append(ge.astype(BF16) * gate)
            wg = jnp.concatenate(ws, axis=0)
            acc_ref[...] += jnp.dot(vt_ref[:, grp], wg, preferred_element_type=F32)

    @pl.when(s % 2 == 0)
    def _():
        step(act1_ref, act0_ref)

    @pl.when(s % 2 == 1)
    def _():
        step(act0_ref, act1_ref)

    @pl.when(jnp.logical_and(s > 0, e_prev == n_blk - 1))
    def _():
        hh = h_ref[...] + acc_ref[...].T
        ms = jnp.mean(hh * hh, axis=-1, keepdims=True)
        o_ref[...] = hh * lax.rsqrt(ms + EPS) * g_ref[...]


def _experts(xnt, u, vt, r2, e2, m, w, h2, g, tm, ec):
    d, n = xnt.shape
    n_exp = u.shape[0]
    heads = r2.shape[0]
    n_blk = n_exp // ec
    assert ec % EXPERT_GROUP == 0 and EXPERT_GROUP % N_KEYS == 0
    steps = (n // tm) * n_blk
    cur = lambda s: jnp.minimum(s, steps - 1)
    prev = lambda s: jnp.maximum(s - 1, 0)
    tile_spec = lambda shape, imap: pl.BlockSpec(shape, imap, pipeline_mode=pl.Buffered(1))
    big = lambda: tile_spec((heads, N_KEYS, tm), lambda s: (0, 0, prev(s) // n_blk))
    return pl.pallas_call(
        functools.partial(_experts_kernel, n_blk),
        grid=(steps + 1,),
        in_specs=[
            pl.BlockSpec((d, tm), lambda s: (0, cur(s) // n_blk)),
            pl.BlockSpec((ec, d), lambda s: (cur(s) % n_blk, 0)),
            pl.BlockSpec((d, ec), lambda s: (0, prev(s) % n_blk)),
            big(), big(), big(), big(),
            tile_spec((tm, d), lambda s: (prev(s) // n_blk, 0)),
            pl.BlockSpec((1, d), lambda s: (0, 0)),
        ],
        out_specs=pl.BlockSpec((tm, d), lambda s: (prev(s) // n_blk, 0)),
        out_shape=jax.ShapeDtypeStruct((n, d), F32),
        scratch_shapes=[pltpu.VMEM((ec, tm), F32), pltpu.VMEM((ec, tm), F32), pltpu.VMEM((d, tm), F32)],
        compiler_params=_params(("arbitrary",)),
        name="peer_experts",
    )(xnt, u, vt, r2, e2, m, w, h2, g)


def _tile(n, want):
    t = min(n, want)
    assert n % t == 0, (n, t)
    return t


def kernel(x, meta, ln1_g, w_in, conv_w, conv_b, rg_wa, rg_ba, rg_wx, rg_bx, rg_lambda, hg_lb_logits,
           hg_norm_g, w_pa, w_pb, w_out, ln2_g, peer_wq, peer_keys, peer_u, peer_v, final_g):
    batch, seq, d = x.shape
    n = batch * seq
    n_meta = meta.shape[0]
    depth = w_in.shape[0]
    assert depth == 1 and seq % TT == 0 and n_meta <= TT
    rg_width = conv_w.shape[2]
    hg_width = hg_norm_g.shape[1]
    assert rg_width == hg_width == d and rg_wa.shape[2] == HEAD
    n_pad = TT - n_meta
    row = lambda a: a.reshape(1, -1)

    xf = x.reshape(n, d)
    meta_tile = jnp.concatenate([jnp.zeros((n_pad, d), x.dtype), meta.astype(x.dtype)], axis=0)

    w_in_b = w_in[0].astype(BF16)
    tm = _tile(n, 1024)
    proj = _norm_matmul(xf, row(ln1_g[0]), w_in_b, tm, 1024)
    proj_meta = _norm_matmul(meta_tile, row(ln1_g[0]), w_in_b, TT, 1024)

    y_a = _rglru(proj, proj_meta, batch, seq, n_pad, conv_w[0], row(conv_b[0]), rg_wa[0].astype(BF16),
                 row(rg_ba[0]), rg_wx[0].astype(BF16), row(rg_bx[0]), row(rg_lambda[0]))
    y_b = _hgrn(proj, proj_meta, batch, seq, n_pad, 2, hg_lb_logits, row(hg_norm_g[0]))

    tn = 1024
    mixed = _merge(y_a, y_b, w_pa[0].astype(BF16), w_pb[0].astype(BF16), proj, 6 * (d // tn), tm, tn)
    h2, xn2t = _outproj(mixed, w_out[0].astype(BF16), xf, row(ln2_g[0]), _tile(n, 512))

    tp = _tile(n, 512)
    r2, e2, m, w = _retrieve(xn2t, peer_wq[0].T.astype(BF16), peer_keys[0].astype(BF16), tp)
    out = _experts(xn2t, peer_u[0].astype(BF16), peer_v[0].astype(BF16).T, r2, e2, m, w, h2,
                   row(final_g), tp, 1024)
    return out.reshape(batch, seq, d)
```

```python
import functools
import math

import jax
import jax.numpy as jnp
from jax import lax
from jax.experimental import pallas as pl
from jax.experimental.pallas import tpu as pltpu

F32 = jnp.float32
BF16 = jnp.bfloat16

EPS = 1e-6
RG_C = 8.0
CONV_WIDTH = 4
HEAD = 128
N_KEYS = 128
PEER_TOPK = 16
TT = 128
RB = 32
LANES = 128
EXPERT_GROUP = 1024
VMEM_LIMIT = 56 * 1024 * 1024

NT_DIMS = (((1,), (1,)), ((), ()))
TN_DIMS = (((0,), (0,)), ((), ()))


def _params(sem):
    return pltpu.CompilerParams(dimension_semantics=sem, vmem_limit_bytes=VMEM_LIMIT)


def _sigmoid(x):
    return jax.nn.sigmoid(x)


def _gelu(x):
    return 0.5 * x * (1.0 + lax.erf(x * (1.0 / math.sqrt(2.0))))


def _norm_mm_kernel(x_ref, g_ref, w_ref, o_ref, xn_ref):
    @pl.when(pl.program_id(1) == 0)
    def _():
        xf = x_ref[...]
        ms = jnp.mean(xf * xf, axis=-1, keepdims=True)
        xn_ref[...] = (xf * lax.rsqrt(ms + EPS) * g_ref[...]).astype(BF16)

    o_ref[...] = jnp.dot(xn_ref[...], w_ref[...], preferred_element_type=F32).astype(o_ref.dtype)


def _norm_matmul(x, g, w, tm, tn):
    m, d = x.shape
    n = w.shape[1]
    return pl.pallas_call(
        _norm_mm_kernel,
        grid=(m // tm, n // tn),
        in_specs=[
            pl.BlockSpec((tm, d), lambda i, j: (i, 0)),
            pl.BlockSpec((1, d), lambda i, j: (0, 0)),
            pl.BlockSpec((d, tn), lambda i, j: (0, j)),
        ],
        out_specs=pl.BlockSpec((tm, tn), lambda i, j: (i, j)),
        out_shape=jax.ShapeDtypeStruct((m, n), BF16),
        scratch_shapes=[pltpu.VMEM((tm, d), BF16)],
        compiler_params=_params(("parallel", "arbitrary")),
        name="norm_matmul",
    )(x, g, w)


def _rglru_kernel(n_pad, xa_ref, ya_ref, xam_ref, yam_ref, cw_ref, cb_ref, wa_ref, ba_ref, wx_ref, bx_ref,
                  lam_ref, o_ref, xbuf_ref, hc_ref):
    t = pl.program_id(1)
    width = xa_ref.shape[1]
    is_meta = t == 0

    @pl.when(is_meta)
    def _():
        xbuf_ref[0:8, :] = jnp.zeros((8, width), F32)
        hc_ref[...] = jnp.zeros_like(hc_ref)

    xbuf_ref[8:8 + TT, :] = jnp.where(is_meta, xam_ref[...], xa_ref[...]).astype(F32)
    row = lax.broadcasted_iota(jnp.int32, (TT, HEAD), 0)
    live = jnp.logical_or(t > 0, row >= n_pad)
    r8 = row & 7

    for h in range(width // HEAD):
        sl = slice(h * HEAD, (h + 1) * HEAD)
        xc = cb_ref[:, sl] + cw_ref[0:1, sl] * xbuf_ref[5:5 + TT, sl]
        for k in range(1, CONV_WIDTH):
            xc = xc + cw_ref[k:k + 1, sl] * xbuf_ref[5 + k:5 + k + TT, sl]
        xcb = xc.astype(BF16)
        r = _sigmoid(jnp.dot(xcb, wa_ref[h], preferred_element_type=F32) + ba_ref[:, sl])
        i = _sigmoid(jnp.dot(xcb, wx_ref[h], preferred_element_type=F32) + bx_ref[:, sl])
        sp = jax.nn.softplus(-lam_ref[:, sl])
        log_a = (-RG_C) * r * sp
        a = jnp.exp(log_a)
        th = jnp.tanh(log_a)
        u = jnp.sqrt(-2.0 * th / (1.0 - th)) * (i * xc)
        u = jnp.where(live, u, 0.0)
        for k in (1, 2, 4):
            a_sh = pltpu.roll(a, k, 0)
            u_sh = pltpu.roll(u, k, 0)
            m = r8 >= k
            u = jnp.where(m, a * u_sh + u, u)
            a = jnp.where(m, a * a_sh, a)
        carry = hc_ref[:, sl]
        outs = []
        for g in range(TT // 8):
            hg = a[g * 8:(g + 1) * 8] * carry + u[g * 8:(g + 1) * 8]
            outs.append(hg)
            carry = hg[7:8]
        hc_ref[:, sl] = carry
        hs = jnp.concatenate(outs, axis=0)
        ya = jnp.where(is_meta, yam_ref[:, sl], ya_ref[:, sl]).astype(F32)
        o_ref[:, sl] = (_gelu(ya) * hs).astype(o_ref.dtype)

    xbuf_ref[0:8, :] = xbuf_ref[TT:TT + 8, :]


def _rglru(proj, proj_meta, batch, seq, n_pad, conv_w, conv_b, wa, ba, wx, bx, lam):
    n = batch * seq
    width = conv_w.shape[1]
    tiles = seq // TT
    in_row = lambda b, t: b * tiles + jnp.maximum(t - 1, 0)

    vec = lambda: pl.BlockSpec((1, width), lambda b, t: (0, 0))
    gate_w = lambda: pl.BlockSpec((width // HEAD, HEAD, HEAD), lambda b, t: (0, 0, 0))
    return pl.pallas_call(
        functools.partial(_rglru_kernel, n_pad),
        grid=(batch, tiles + 1),
        in_specs=[
            pl.BlockSpec((TT, width), lambda b, t: (in_row(b, t), 0)),
            pl.BlockSpec((TT, width), lambda b, t: (in_row(b, t), 1)),
            pl.BlockSpec((TT, width), lambda b, t: (0, 0)),
            pl.BlockSpec((TT, width), lambda b, t: (0, 1)),
            pl.BlockSpec((CONV_WIDTH, width), lambda b, t: (0, 0)),
            vec(), gate_w(), vec(), gate_w(), vec(), vec(),
        ],
        out_specs=pl.BlockSpec((TT, width), lambda b, t: (in_row(b, t), 0)),
        out_shape=jax.ShapeDtypeStruct((n, width), BF16),
        scratch_shapes=[pltpu.VMEM((TT + 8, width), F32), pltpu.VMEM((1, width), F32)],
        compiler_params=_params(("parallel", "arbitrary")),
        name="rglru",
    )(proj, proj, proj_meta, proj_meta, conv_w, conv_b, wa, ba, wx, bx, lam)


def _hgrn_kernel(n_pad, q_ref, f_ref, v_ref, g_ref, qm_ref, fm_ref, vm_ref, gm_ref, lbl_ref, ng_ref, o_ref,
                 st_ref):
    t = pl.program_id(1)
    width = q_ref.shape[1]
    is_meta = t == 0

    @pl.when(is_meta)
    def _():
        st_ref[...] = jnp.zeros_like(st_ref)

    pick = lambda meta_ref, ref: jnp.where(is_meta, meta_ref[...], ref[...])
    q_in, f_in, v_in, g_in = pick(qm_ref, q_ref), pick(fm_ref, f_ref), pick(vm_ref, v_ref), pick(gm_ref, g_ref)

    heads = [slice(h * HEAD, (h + 1) * HEAD) for h in range(width // HEAD)]
    blocks = [(r0, r0 + RB, r0 + RB // 2 - 1) for r0 in range(0, TT, RB)]

    row = lax.broadcasted_iota(jnp.int32, (TT, width), 0)
    live = jnp.logical_or(t > 0, row >= n_pad)
    lg = lbl_ref[...]
    e = jnp.exp(lg - jnp.max(lg, axis=0, keepdims=True))
    lb = e[0:1] / jnp.sum(e, axis=0, keepdims=True)
    f = lb + (1.0 - lb) * _sigmoid(f_in.astype(F32))
    kk = 1.0 - f
    bcum = jnp.where(live, jnp.log(f), 0.0)
    k = 1
    while k < TT:
        bcum = bcum + jnp.where(row >= k, pltpu.roll(bcum, k, 0), 0.0)
        k *= 2
    bend = bcum[TT - 1:TT]
    q = q_in.astype(F32)
    qs = q * _sigmoid(q)
    qhat = (qs * jnp.exp(bcum)).astype(BF16)
    khat = (kk * jnp.exp(bend - bcum)).astype(BF16)
    dec = jnp.exp(bend)
    qts, kts = [], []
    for r0, r1, mid in blocks:
        bref = bcum[mid:mid + 1]
        qts.append((qs[r0:r1] * jnp.exp(bcum[r0:r1] - bref)).astype(BF16))
        kt = kk * jnp.exp(bref - bcum)
        kts.append((kt if r1 == TT else jnp.where(row < r1, kt, 0.0)).astype(BF16))

    tri = lax.broadcasted_iota(jnp.int32, (TT, TT), 0) >= lax.broadcasted_iota(jnp.int32, (TT, TT), 1)
    o_inter = [lax.dot_general(qhat[:, sl], st_ref[h].astype(BF16), NT_DIMS, preferred_element_type=F32)
               for h, sl in enumerate(heads)]
    scores = [[lax.dot_general(qt[:, sl], kt[:, sl], NT_DIMS, preferred_element_type=F32)
               for qt, kt in zip(qts, kts)] for sl in heads]
    outs = []
    for h, sl in enumerate(heads):
        p = jnp.where(tri, jnp.concatenate(scores[h], axis=0), 0.0).astype(BF16)
        outs.append(jnp.dot(p, v_in[:, sl], preferred_element_type=F32) + o_inter[h])
    for h, sl in enumerate(heads):
        st_ref[h] = st_ref[h] * dec[:, sl] + lax.dot_general(v_in[:, sl], khat[:, sl], TN_DIMS,
                                                             preferred_element_type=F32)
    for h, sl in enumerate(heads):
        o = outs[h]
        o = o * lax.rsqrt(jnp.mean(o * o, axis=-1, keepdims=True) + EPS)
        og = g_in[:, sl].astype(F32)
        o_ref[:, sl] = (o * ng_ref[:, sl] * (og * _sigmoid(og))).astype(o_ref.dtype)


def _hgrn(proj, proj_meta, batch, seq, n_pad, col0, lb_logits, norm_g):
    n = batch * seq
    width = norm_g.shape[1]
    tiles = seq // TT
    in_row = lambda b, t: b * tiles + jnp.maximum(t - 1, 0)

    col = lambda c: pl.BlockSpec((TT, width), lambda b, t: (in_row(b, t), col0 + c))
    meta_col = lambda c: pl.BlockSpec((TT, width), lambda b, t: (0, col0 + c))
    return pl.pallas_call(
        functools.partial(_hgrn_kernel, n_pad),
        grid=(batch, tiles + 1),
        in_specs=[
            col(0), col(1), col(2), col(3),
            meta_col(0), meta_col(1), meta_col(2), meta_col(3),
            pl.BlockSpec(lb_logits.shape, lambda b, t: (0, 0)),
            pl.BlockSpec((1, width), lambda b, t: (0, 0)),
        ],
        out_specs=pl.BlockSpec((TT, width), lambda b, t: (in_row(b, t), 0)),
        out_shape=jax.ShapeDtypeStruct((n, width), BF16),
        scratch_shapes=[pltpu.VMEM((width // HEAD, HEAD, HEAD), F32)],
        compiler_params=_params(("parallel", "arbitrary")),
        name="hgrn2",
    )(proj, proj, proj, proj, proj_meta, proj_meta, proj_meta, proj_meta, lb_logits, norm_g)


def _merge_kernel(ya_ref, yb_ref, wpa_ref, wpb_ref, za_ref, zb_ref, o_ref):
    ta = jnp.dot(ya_ref[...], wpa_ref[...], preferred_element_type=F32)
    tb = jnp.dot(yb_ref[...], wpb_ref[...], preferred_element_type=F32)
    mixed = _sigmoid(za_ref[...].astype(F32)) * ta + _sigmoid(zb_ref[...].astype(F32)) * tb
    o_ref[...] = mixed.astype(o_ref.dtype)


def _merge(ya, yb, wpa, wpb, proj, zcol0, tm, tn):
    n, d = ya.shape
    dm = wpa.shape[1]
    nj = dm // tn
    return pl.pallas_call(
        _merge_kernel,
        grid=(n // tm, nj),
        in_specs=[
            pl.BlockSpec((tm, d), lambda i, j: (i, 0)),
            pl.BlockSpec((tm, d), lambda i, j: (i, 0)),
            pl.BlockSpec((d, tn), lambda i, j: (0, j)),
            pl.BlockSpec((d, tn), lambda i, j: (0, j)),
            pl.BlockSpec((tm, tn), lambda i, j: (i, zcol0 + j)),
            pl.BlockSpec((tm, tn), lambda i, j: (i, zcol0 + nj + j)),
        ],
        out_specs=pl.BlockSpec((tm, tn), lambda i, j: (i, j)),
        out_shape=jax.ShapeDtypeStruct((n, dm), BF16),
        compiler_params=_params(("parallel", "arbitrary")),
        name="merge",
    )(ya, yb, wpa, wpb, proj, proj)


def _outproj_kernel(m_ref, w_ref, x_ref, g_ref, h_ref, xnt_ref):
    h = x_ref[...] + jnp.dot(m_ref[...], w_ref[...], preferred_element_type=F32)
    h_ref[...] = h
    ms = jnp.mean(h * h, axis=-1, keepdims=True)
    xn = h * lax.rsqrt(ms + EPS) * g_ref[...]
    xnt_ref[...] = xn.T.astype(xnt_ref.dtype)


def _outproj(mixed, w_out, x, g, tm):
    n, d = x.shape
    return pl.pallas_call(
        _outproj_kernel,
        grid=(n // tm,),
        in_specs=[
            pl.BlockSpec((tm, d), lambda i: (i, 0)),
            pl.BlockSpec((d, d), lambda i: (0, 0)),
            pl.BlockSpec((tm, d), lambda i: (i, 0)),
            pl.BlockSpec((1, d), lambda i: (0, 0)),
        ],
        out_specs=[pl.BlockSpec((tm, d), lambda i: (i, 0)), pl.BlockSpec((d, tm), lambda i: (0, i))],
        out_shape=[jax.ShapeDtypeStruct((n, d), F32), jax.ShapeDtypeStruct((d, n), BF16)],
        compiler_params=_params(("parallel",)),
        name="outproj",
    )(mixed, w_out, x, g)


def _odd_even_merge_sort_pairs(n):
    pairs = []

    def merge(lo, cnt, r):
        step = r * 2
        if step < cnt:
            merge(lo, cnt, step)
            merge(lo + r, cnt, step)
            pairs.extend((i, i + r) for i in range(lo + r, lo + cnt - r, step))
        else:
            pairs.append((lo, lo + r))

    def sort(lo, cnt):
        if cnt > 1:
            sort(lo, cnt // 2)
            sort(lo + cnt // 2, cnt // 2)
            merge(lo, cnt, 1)

    sort(0, n)
    return pairs


def _order(v, i, j):
    v[i], v[j] = jnp.maximum(v[i], v[j]), jnp.minimum(v[i], v[j])


def _top_sorted(s):
    n = s.shape[0] // 8
    v = [s[8 * i:8 * (i + 1)] for i in range(n)]
    for i, j in _odd_even_merge_sort_pairs(n):
        _order(v, i, j)
    for shift in (4, 2, 1):
        v = [jnp.maximum(v[i], pltpu.roll(v[n - 1 - i], shift, 0)) for i in range(n)]
        d = n // 2
        while d:
            for i in range(n):
                if not i & d:
                    _order(v, i, i + d)
            d //= 2
    return v


def _prefix_len(t, test):
    b3 = test(t[7])
    b2 = test(jnp.where(b3, t[11], t[3]))
    b1 = test(jnp.where(b3, jnp.where(b2, t[13], t[9]), jnp.where(b2, t[5], t[1])))
    hi = jnp.where(b2, jnp.where(b1, t[14], t[12]), jnp.where(b1, t[10], t[8]))
    lo = jnp.where(b2, jnp.where(b1, t[6], t[4]), jnp.where(b1, t[2], t[0]))
    b0 = test(jnp.where(b3, hi, lo))
    p = jnp.where(b3, 8.0, 0.0) + jnp.where(b2, 4.0, 0.0) + jnp.where(b1, 2.0, 0.0) + jnp.where(b0, 1.0, 0.0)
    return p + jnp.where(test(t[15]), 1.0, 0.0)


def _retrieve_kernel(xnt_ref, wqt_ref, keys_ref, r2_ref, e2_ref, m_ref, w_ref):
    heads = keys_ref.shape[0]
    dq = keys_ref.shape[3]
    k = PEER_TOPK
    tm = xnt_ref.shape[1]
    for h in range(heads):
        qt = jnp.dot(wqt_ref[2 * h * dq:(2 * h + 2) * dq, :], xnt_ref[...], preferred_element_type=F32)
        qt = qt.astype(BF16)
        s1_all = jnp.dot(keys_ref[h, 0], qt[:dq], preferred_element_type=F32)
        s2_all = jnp.dot(keys_ref[h, 1], qt[dq:], preferred_element_type=F32)
        for lo in range(0, tm, LANES):
            cols = slice(lo, lo + LANES)
            s1, s2 = s1_all[:, cols], s2_all[:, cols]
            top1, top2 = _top_sorted(s1), _top_sorted(s2)
            t1 = [t[0:1] for t in top1]
            t2 = [t[0:1] for t in top2]
            t2s = jnp.concatenate(t2, axis=0)
            pieces = [t1[i] + t2s[:k // (i + 1)] for i in range(k)]
            rows = sum(p.shape[0] for p in pieces)
            pieces.append(jnp.full((-rows % 8, LANES), -jnp.inf, F32))
            work = jnp.concatenate(pieces, axis=0)
            tau = None
            for _ in range(k):
                tau = jnp.max(work, axis=0, keepdims=True)
                work = jnp.where(work == tau, -jnp.inf, work)
            e2s = jnp.exp(t2s - t2[0])
            z = jnp.zeros_like(tau)
            for i in range(k):
                sel = (t1[i] + t2s) >= tau
                z = z + jnp.exp(t1[i] - t1[0]) * jnp.sum(jnp.where(sel, e2s, 0.0), axis=0, keepdims=True)
            rows8 = [slice(8 * i, 8 * (i + 1)) for i in range(N_KEYS // 8)]
            r2 = jnp.concatenate([_prefix_len(top2, lambda t, x=s2[r]: t > x) for r in rows8], axis=0)
            m = jnp.concatenate([_prefix_len(top2, lambda t, x=s1[r]: (x + t) >= tau) for r in rows8], axis=0)
            r2_ref[h, :, cols] = r2.astype(r2_ref.dtype)
            e2_ref[h, :, cols] = jnp.exp(s2 - t2[0]).astype(e2_ref.dtype)
            m_ref[h, :, cols] = m
            w_ref[h, :, cols] = (0.5 * jnp.exp(s1 - t1[0])) / z


def _retrieve(xnt, wqt, keys, tm):
    d, n = xnt.shape
    heads = keys.shape[0]
    assert tm % LANES == 0
    assert PEER_TOPK == 16 and keys.shape[2] == N_KEYS == 8 * PEER_TOPK
    big = lambda: pl.BlockSpec((heads, N_KEYS, tm), lambda i: (0, 0, i))
    shape = lambda dt: jax.ShapeDtypeStruct((heads, N_KEYS, n), dt)
    return pl.pallas_call(
        _retrieve_kernel,
        grid=(n // tm,),
        in_specs=[
            pl.BlockSpec((d, tm), lambda i: (0, i)),
            pl.BlockSpec(wqt.shape, lambda i: (0, 0)),
            pl.BlockSpec(keys.shape, lambda i: (0, 0, 0, 0)),
        ],
        out_specs=[big(), big(), big(), big()],
        out_shape=[shape(BF16), shape(BF16), shape(F32), shape(F32)],
        compiler_params=_params(("parallel",)),
        name="peer_retrieve",
    )(xnt, wqt, keys)


def _experts_kernel(n_blk, xnt_ref, u_ref, vt_ref, r2_ref, e2_ref, m_ref, w_ref, h_ref, g_ref, o_ref,
                    act0_ref, act1_ref, acc_ref):
    s = pl.program_id(0)
    heads = r2_ref.shape[0]
    ec = u_ref.shape[0]
    e_prev = jnp.maximum(s - 1, 0) % n_blk

    @pl.when(s == 0)
    def _():
        act1_ref[...] = jnp.zeros_like(act1_ref)

    @pl.when(e_prev == 0)
    def _():
        acc_ref[...] = jnp.zeros_like(acc_ref)

    def step(src_ref, dst_ref):
        c0 = e_prev * (ec // N_KEYS)
        mrows = [[m_ref[h, pl.ds(c0 + cc, 1), :].astype(BF16) for h in range(heads)]
                 for cc in range(ec // N_KEYS)]
        wrows = [[w_ref[h, pl.ds(c0 + cc, 1), :].astype(BF16) for h in range(heads)]
                 for cc in range(ec // N_KEYS)]
        for k0 in range(0, ec, EXPERT_GROUP):
            grp = slice(k0, k0 + EXPERT_GROUP)
            dst_ref[grp, :] = jnp.dot(u_ref[grp, :], xnt_ref[...], preferred_element_type=F32)
            ws = []
            for r0 in range(k0, k0 + EXPERT_GROUP, N_KEYS):
                gate = None
                for h in range(heads):
                    gh = jnp.where(r2_ref[h] < mrows[r0 // N_KEYS][h], e2_ref[h], 0) * wrows[r0 // N_KEYS][h]
                    gate = gh if gate is None else gate + gh
                a = src_ref[r0:r0 + N_KEYS, :]
                ge = a * (1.0 + lax.erf(a * (1.0 / math.sqrt(2.0))))
                ws.append(ge.astype(BF16) * gate)
            wg = jnp.concatenate(ws, axis=0)
            acc_ref[...] += jnp.dot(vt_ref[:, grp], wg, preferred_element_type=F32)

    @pl.when(s % 2 == 0)
    def _():
        step(act1_ref, act0_ref)

    @pl.when(s % 2 == 1)
    def _():
        step(act0_ref, act1_ref)

    @pl.when(jnp.logical_and(s > 0, e_prev == n_blk - 1))
    def _():
        hh = h_ref[...] + acc_ref[...].T
        ms = jnp.mean(hh * hh, axis=-1, keepdims=True)
        o_ref[...] = hh * lax.rsqrt(ms + EPS) * g_ref[...]


def _experts(xnt, u, vt, r2, e2, m, w, h2, g, tm, ec):
    d, n = xnt.shape
    n_exp = u.shape[0]
    heads = r2.shape[0]
    n_blk = n_exp // ec
    assert ec % EXPERT_GROUP == 0 and EXPERT_GROUP % N_KEYS == 0
    steps = (n // tm) * n_blk
    cur = lambda s: jnp.minimum(s, steps - 1)
    prev = lambda s: jnp.maximum(s - 1, 0)
    tile_spec = lambda shape, imap: pl.BlockSpec(shape, imap, pipeline_mode=pl.Buffered(1))
    big = lambda: tile_spec((heads, N_KEYS, tm), lambda s: (0, 0, prev(s) // n_blk))
    return pl.pallas_call(
        functools.partial(_experts_kernel, n_blk),
        grid=(steps + 1,),
        in_specs=[
            pl.BlockSpec((d, tm), lambda s: (0, cur(s) // n_blk)),
            pl.BlockSpec((ec, d), lambda s: (cur(s) % n_blk, 0)),
            pl.BlockSpec((d, ec), lambda s: (0, prev(s) % n_blk)),
            big(), big(), big(), big(),
            tile_spec((tm, d), lambda s: (prev(s) // n_blk, 0)),
            pl.BlockSpec((1, d), lambda s: (0, 0)),
        ],
        out_specs=pl.BlockSpec((tm, d), lambda s: (prev(s) // n_blk, 0)),
        out_shape=jax.ShapeDtypeStruct((n, d), F32),
        scratch_shapes=[pltpu.VMEM((ec, tm), F32), pltpu.VMEM((ec, tm), F32), pltpu.VMEM((d, tm), F32)],
        compiler_params=_params(("arbitrary",)),
        name="peer_experts",
    )(xnt, u, vt, r2, e2, m, w, h2, g)


def _transpose_cast_kernel(x_ref, o_ref):
    o_ref[...] = x_ref[...].T.astype(o_ref.dtype)


def _transpose_cast(x, tr):
    rows, cols = x.shape
    return pl.pallas_call(
        _transpose_cast_kernel,
        grid=(rows // tr,),
        in_specs=[pl.BlockSpec((tr, cols), lambda i: (i, 0))],
        out_specs=pl.BlockSpec((cols, tr), lambda i: (0, i)),
        out_shape=jax.ShapeDtypeStruct((cols, rows), BF16),
        compiler_params=_params(("parallel",)),
        name="transpose_cast",
    )(x)


def _tile(n, want):
    t = min(n, want)
    assert n % t == 0, (n, t)
    return t


def kernel(x, meta, ln1_g, w_in, conv_w, conv_b, rg_wa, rg_ba, rg_wx, rg_bx, rg_lambda, hg_lb_logits,
           hg_norm_g, w_pa, w_pb, w_out, ln2_g, peer_wq, peer_keys, peer_u, peer_v, final_g):
    batch, seq, d = x.shape
    n = batch * seq
    n_meta = meta.shape[0]
    depth = w_in.shape[0]
    assert depth == 1 and seq % TT == 0 and n_meta <= TT
    rg_width = conv_w.shape[2]
    hg_width = hg_norm_g.shape[1]
    assert rg_width == hg_width == d and rg_wa.shape[2] == HEAD
    n_pad = TT - n_meta
    row = lambda a: a.reshape(1, -1)

    xf = x.reshape(n, d)
    meta_tile = jnp.concatenate([jnp.zeros((n_pad, d), x.dtype), meta.astype(x.dtype)], axis=0)

    w_in_b = w_in[0].astype(BF16)
    tm = _tile(n, 1024)
    proj = _norm_matmul(xf, row(ln1_g[0]), w_in_b, tm, 1024)
    proj_meta = _norm_matmul(meta_tile, row(ln1_g[0]), w_in_b, TT, 1024)

    y_a = _rglru(proj, proj_meta, batch, seq, n_pad, conv_w[0], row(conv_b[0]), rg_wa[0].astype(BF16),
                 row(rg_ba[0]), rg_wx[0].astype(BF16), row(rg_bx[0]), row(rg_lambda[0]))
    y_b = _hgrn(proj, proj_meta, batch, seq, n_pad, 2, hg_lb_logits, row(hg_norm_g[0]))

    tn = 1024
    mixed = _merge(y_a, y_b, w_pa[0].astype(BF16), w_pb[0].astype(BF16), proj, 6 * (d // tn), tm, tn)
    h2, xn2t = _outproj(mixed, w_out[0].astype(BF16), xf, row(ln2_g[0]), _tile(n, 512))

    tp = _tile(n, 512)
    r2, e2, m, w = _retrieve(xn2t, peer_wq[0].T.astype(BF16), peer_keys[0].astype(BF16), tp)
    out = _experts(xn2t, peer_u[0].astype(BF16), _transpose_cast(peer_v[0], 512), r2, e2, m, w, h2,
                   row(final_g), tp, 1024)
    return out.reshape(batch, seq, d)
```

```python
import functools
import math

import jax
import jax.numpy as jnp
from jax import lax
from jax.experimental import pallas as pl
from jax.experimental.pallas import tpu as pltpu

F32 = jnp.float32
BF16 = jnp.bfloat16

EPS = 1e-6
RG_C = 8.0
CONV_WIDTH = 4
HEAD = 128
N_KEYS = 128
PEER_TOPK = 16
TT = 128
RB = 32
LANES = 128
EXPERT_GROUP = 1024
VMEM_LIMIT = 56 * 1024 * 1024

NT_DIMS = (((1,), (1,)), ((), ()))
TN_DIMS = (((0,), (0,)), ((), ()))


def _params(sem):
    return pltpu.CompilerParams(dimension_semantics=sem, vmem_limit_bytes=VMEM_LIMIT)


def _sigmoid(x):
    return jax.nn.sigmoid(x)


def _gelu(x):
    return 0.5 * x * (1.0 + lax.erf(x * (1.0 / math.sqrt(2.0))))


def _norm_mm_kernel(x_ref, g_ref, w_ref, o_ref, xn_ref):
    @pl.when(pl.program_id(1) == 0)
    def _():
        xf = x_ref[...]
        ms = jnp.mean(xf * xf, axis=-1, keepdims=True)
        xn_ref[...] = (xf * lax.rsqrt(ms + EPS) * g_ref[...]).astype(BF16)

    o_ref[...] = jnp.dot(xn_ref[...], w_ref[...].astype(BF16), preferred_element_type=F32).astype(o_ref.dtype)


def _norm_matmul(x, g, w, tm, tn):
    m, d = x.shape
    n = w.shape[1]
    return pl.pallas_call(
        _norm_mm_kernel,
        grid=(m // tm, n // tn),
        in_specs=[
            pl.BlockSpec((tm, d), lambda i, j: (i, 0)),
            pl.BlockSpec((1, d), lambda i, j: (0, 0)),
            pl.BlockSpec((d, tn), lambda i, j: (0, j)),
        ],
        out_specs=pl.BlockSpec((tm, tn), lambda i, j: (i, j)),
        out_shape=jax.ShapeDtypeStruct((m, n), BF16),
        scratch_shapes=[pltpu.VMEM((tm, d), BF16)],
        compiler_params=_params(("parallel", "arbitrary")),
        name="norm_matmul",
    )(x, g, w)


def _rglru_kernel(n_pad, xa_ref, ya_ref, xam_ref, yam_ref, cw_ref, cb_ref, wa_ref, ba_ref, wx_ref, bx_ref,
                  lam_ref, o_ref, xbuf_ref, hc_ref):
    t = pl.program_id(1)
    width = xa_ref.shape[1]
    is_meta = t == 0

    @pl.when(is_meta)
    def _():
        xbuf_ref[0:8, :] = jnp.zeros((8, width), F32)
        hc_ref[...] = jnp.zeros_like(hc_ref)

    xbuf_ref[8:8 + TT, :] = jnp.where(is_meta, xam_ref[...], xa_ref[...]).astype(F32)
    row = lax.broadcasted_iota(jnp.int32, (TT, HEAD), 0)
    live = jnp.logical_or(t > 0, row >= n_pad)
    r8 = row & 7

    for h in range(width // HEAD):
        sl = slice(h * HEAD, (h + 1) * HEAD)
        xc = cb_ref[:, sl] + cw_ref[0:1, sl] * xbuf_ref[5:5 + TT, sl]
        for k in range(1, CONV_WIDTH):
            xc = xc + cw_ref[k:k + 1, sl] * xbuf_ref[5 + k:5 + k + TT, sl]
        xcb = xc.astype(BF16)
        r = _sigmoid(jnp.dot(xcb, wa_ref[h], preferred_element_type=F32) + ba_ref[:, sl])
        i = _sigmoid(jnp.dot(xcb, wx_ref[h], preferred_element_type=F32) + bx_ref[:, sl])
        sp = jax.nn.softplus(-lam_ref[:, sl])
        log_a = (-RG_C) * r * sp
        a = jnp.exp(log_a)
        th = jnp.tanh(log_a)
        u = jnp.sqrt(-2.0 * th / (1.0 - th)) * (i * xc)
        u = jnp.where(live, u, 0.0)
        for k in (1, 2, 4):
            a_sh = pltpu.roll(a, k, 0)
            u_sh = pltpu.roll(u, k, 0)
            m = r8 >= k
            u = jnp.where(m, a * u_sh + u, u)
            a = jnp.where(m, a * a_sh, a)
        carry = hc_ref[:, sl]
        outs = []
        for g in range(TT // 8):
            hg = a[g * 8:(g + 1) * 8] * carry + u[g * 8:(g + 1) * 8]
            outs.append(hg)
            carry = hg[7:8]
        hc_ref[:, sl] = carry
        hs = jnp.concatenate(outs, axis=0)
        ya = jnp.where(is_meta, yam_ref[:, sl], ya_ref[:, sl]).astype(F32)
        o_ref[:, sl] = (_gelu(ya) * hs).astype(o_ref.dtype)

    xbuf_ref[0:8, :] = xbuf_ref[TT:TT + 8, :]


def _rglru(proj, proj_meta, batch, seq, n_pad, conv_w, conv_b, wa, ba, wx, bx, lam):
    n = batch * seq
    width = conv_w.shape[1]
    tiles = seq // TT
    in_row = lambda b, t: b * tiles + jnp.maximum(t - 1, 0)

    vec = lambda: pl.BlockSpec((1, width), lambda b, t: (0, 0))
    gate_w = lambda: pl.BlockSpec((width // HEAD, HEAD, HEAD), lambda b, t: (0, 0, 0))
    return pl.pallas_call(
        functools.partial(_rglru_kernel, n_pad),
        grid=(batch, tiles + 1),
        in_specs=[
            pl.BlockSpec((TT, width), lambda b, t: (in_row(b, t), 0)),
            pl.BlockSpec((TT, width), lambda b, t: (in_row(b, t), 1)),
            pl.BlockSpec((TT, width), lambda b, t: (0, 0)),
            pl.BlockSpec((TT, width), lambda b, t: (0, 1)),
            pl.BlockSpec((CONV_WIDTH, width), lambda b, t: (0, 0)),
            vec(), gate_w(), vec(), gate_w(), vec(), vec(),
        ],
        out_specs=pl.BlockSpec((TT, width), lambda b, t: (in_row(b, t), 0)),
        out_shape=jax.ShapeDtypeStruct((n, width), BF16),
        scratch_shapes=[pltpu.VMEM((TT + 8, width), F32), pltpu.VMEM((1, width), F32)],
        compiler_params=_params(("parallel", "arbitrary")),
        name="rglru",
    )(proj, proj, proj_meta, proj_meta, conv_w, conv_b, wa, ba, wx, bx, lam)


def _hgrn_kernel(n_pad, q_ref, f_ref, v_ref, g_ref, qm_ref, fm_ref, vm_ref, gm_ref, lbl_ref, ng_ref, o_ref,
                 st_ref):
    t = pl.program_id(1)
    width = q_ref.shape[1]
    is_meta = t == 0

    @pl.when(is_meta)
    def _():
        st_ref[...] = jnp.zeros_like(st_ref)

    pick = lambda meta_ref, ref: jnp.where(is_meta, meta_ref[...], ref[...])
    q_in, f_in, v_in, g_in = pick(qm_ref, q_ref), pick(fm_ref, f_ref), pick(vm_ref, v_ref), pick(gm_ref, g_ref)

    heads = [slice(h * HEAD, (h + 1) * HEAD) for h in range(width // HEAD)]
    blocks = [(r0, r0 + RB, r0 + RB // 2 - 1) for r0 in range(0, TT, RB)]

    row = lax.broadcasted_iota(jnp.int32, (TT, width), 0)
    live = jnp.logical_or(t > 0, row >= n_pad)
    lg = lbl_ref[...]
    e = jnp.exp(lg - jnp.max(lg, axis=0, keepdims=True))
    lb = e[0:1] / jnp.sum(e, axis=0, keepdims=True)
    f = lb + (1.0 - lb) * _sigmoid(f_in.astype(F32))
    kk = 1.0 - f
    bcum = jnp.where(live, jnp.log(f), 0.0)
    k = 1
    while k < TT:
        bcum = bcum + jnp.where(row >= k, pltpu.roll(bcum, k, 0), 0.0)
        k *= 2
    bend = bcum[TT - 1:TT]
    q = q_in.astype(F32)
    qs = q * _sigmoid(q)
    qhat = (qs * jnp.exp(bcum)).astype(BF16)
    khat = (kk * jnp.exp(bend - bcum)).astype(BF16)
    dec = jnp.exp(bend)
    qts, kts = [], []
    for r0, r1, mid in blocks:
        bref = bcum[mid:mid + 1]
        qts.append((qs[r0:r1] * jnp.exp(bcum[r0:r1] - bref)).astype(BF16))
        kt = kk * jnp.exp(bref - bcum)
        kts.append((kt if r1 == TT else jnp.where(row < r1, kt, 0.0)).astype(BF16))

    tri = lax.broadcasted_iota(jnp.int32, (TT, TT), 0) >= lax.broadcasted_iota(jnp.int32, (TT, TT), 1)
    o_inter = [lax.dot_general(qhat[:, sl], st_ref[h].astype(BF16), NT_DIMS, preferred_element_type=F32)
               for h, sl in enumerate(heads)]
    scores = [[lax.dot_general(qt[:, sl], kt[:, sl], NT_DIMS, preferred_element_type=F32)
               for qt, kt in zip(qts, kts)] for sl in heads]
    outs = []
    for h, sl in enumerate(heads):
        p = jnp.where(tri, jnp.concatenate(scores[h], axis=0), 0.0).astype(BF16)
        outs.append(jnp.dot(p, v_in[:, sl], preferred_element_type=F32) + o_inter[h])
    for h, sl in enumerate(heads):
        st_ref[h] = st_ref[h] * dec[:, sl] + lax.dot_general(v_in[:, sl], khat[:, sl], TN_DIMS,
                                                             preferred_element_type=F32)
    for h, sl in enumerate(heads):
        o = outs[h]
        o = o * lax.rsqrt(jnp.mean(o * o, axis=-1, keepdims=True) + EPS)
        og = g_in[:, sl].astype(F32)
        o_ref[:, sl] = (o * ng_ref[:, sl] * (og * _sigmoid(og))).astype(o_ref.dtype)


def _hgrn(proj, proj_meta, batch, seq, n_pad, col0, lb_logits, norm_g):
    n = batch * seq
    width = norm_g.shape[1]
    tiles = seq // TT
    in_row = lambda b, t: b * tiles + jnp.maximum(t - 1, 0)

    col = lambda c: pl.BlockSpec((TT, width), lambda b, t: (in_row(b, t), col0 + c))
    meta_col = lambda c: pl.BlockSpec((TT, width), lambda b, t: (0, col0 + c))
    return pl.pallas_call(
        functools.partial(_hgrn_kernel, n_pad),
        grid=(batch, tiles + 1),
        in_specs=[
            col(0), col(1), col(2), col(3),
            meta_col(0), meta_col(1), meta_col(2), meta_col(3),
            pl.BlockSpec(lb_logits.shape, lambda b, t: (0, 0)),
            pl.BlockSpec((1, width), lambda b, t: (0, 0)),
        ],
        out_specs=pl.BlockSpec((TT, width), lambda b, t: (in_row(b, t), 0)),
        out_shape=jax.ShapeDtypeStruct((n, width), BF16),
        scratch_shapes=[pltpu.VMEM((width // HEAD, HEAD, HEAD), F32)],
        compiler_params=_params(("parallel", "arbitrary")),
        name="hgrn2",
    )(proj, proj, proj, proj, proj_meta, proj_meta, proj_meta, proj_meta, lb_logits, norm_g)


def _merge_kernel(ya_ref, yb_ref, wpa_ref, wpb_ref, za_ref, zb_ref, o_ref):
    ta = jnp.dot(ya_ref[...], wpa_ref[...], preferred_element_type=F32)
    tb = jnp.dot(yb_ref[...], wpb_ref[...], preferred_element_type=F32)
    mixed = _sigmoid(za_ref[...].astype(F32)) * ta + _sigmoid(zb_ref[...].astype(F32)) * tb
    o_ref[...] = mixed.astype(o_ref.dtype)


def _merge(ya, yb, wpa, wpb, proj, zcol0, tm, tn):
    n, d = ya.shape
    dm = wpa.shape[1]
    nj = dm // tn
    return pl.pallas_call(
        _merge_kernel,
        grid=(n // tm, nj),
        in_specs=[
            pl.BlockSpec((tm, d), lambda i, j: (i, 0)),
            pl.BlockSpec((tm, d), lambda i, j: (i, 0)),
            pl.BlockSpec((d, tn), lambda i, j: (0, j)),
            pl.BlockSpec((d, tn), lambda i, j: (0, j)),
            pl.BlockSpec((tm, tn), lambda i, j: (i, zcol0 + j)),
            pl.BlockSpec((tm, tn), lambda i, j: (i, zcol0 + nj + j)),
        ],
        out_specs=pl.BlockSpec((tm, tn), lambda i, j: (i, j)),
        out_shape=jax.ShapeDtypeStruct((n, dm), BF16),
        compiler_params=_params(("parallel", "arbitrary")),
        name="merge",
    )(ya, yb, wpa, wpb, proj, proj)


def _outproj_kernel(m_ref, w_ref, x_ref, g_ref, h_ref, xnt_ref):
    h = x_ref[...] + jnp.dot(m_ref[...], w_ref[...], preferred_element_type=F32)
    h_ref[...] = h
    ms = jnp.mean(h * h, axis=-1, keepdims=True)
    xn = h * lax.rsqrt(ms + EPS) * g_ref[...]
    xnt_ref[...] = xn.T.astype(xnt_ref.dtype)


def _outproj(mixed, w_out, x, g, tm):
    n, d = x.shape
    return pl.pallas_call(
        _outproj_kernel,
        grid=(n // tm,),
        in_specs=[
            pl.BlockSpec((tm, d), lambda i: (i, 0)),
            pl.BlockSpec((d, d), lambda i: (0, 0)),
            pl.BlockSpec((tm, d), lambda i: (i, 0)),
            pl.BlockSpec((1, d), lambda i: (0, 0)),
        ],
        out_specs=[pl.BlockSpec((tm, d), lambda i: (i, 0)), pl.BlockSpec((d, tm), lambda i: (0, i))],
        out_shape=[jax.ShapeDtypeStruct((n, d), F32), jax.ShapeDtypeStruct((d, n), BF16)],
        compiler_params=_params(("parallel",)),
        name="outproj",
    )(mixed, w_out, x, g)


def _odd_even_merge_sort_pairs(n):
    pairs = []

    def merge(lo, cnt, r):
        step = r * 2
        if step < cnt:
            merge(lo, cnt, step)
            merge(lo + r, cnt, step)
            pairs.extend((i, i + r) for i in range(lo + r, lo + cnt - r, step))
        else:
            pairs.append((lo, lo + r))

    def sort(lo, cnt):
        if cnt > 1:
            sort(lo, cnt // 2)
            sort(lo + cnt // 2, cnt // 2)
            merge(lo, cnt, 1)

    sort(0, n)
    return pairs


def _order(v, i, j):
    v[i], v[j] = jnp.maximum(v[i], v[j]), jnp.minimum(v[i], v[j])


def _top_sorted(s):
    n = s.shape[0] // 8
    v = [s[8 * i:8 * (i + 1)] for i in range(n)]
    for i, j in _odd_even_merge_sort_pairs(n):
        _order(v, i, j)
    for shift in (4, 2, 1):
        v = [jnp.maximum(v[i], pltpu.roll(v[n - 1 - i], shift, 0)) for i in range(n)]
        d = n // 2
        while d:
            for i in range(n):
                if not i & d:
                    _order(v, i, i + d)
            d //= 2
    return v


def _prefix_len(t, test):
    b3 = test(t[7])
    b2 = test(jnp.where(b3, t[11], t[3]))
    b1 = test(jnp.where(b3, jnp.where(b2, t[13], t[9]), jnp.where(b2, t[5], t[1])))
    hi = jnp.where(b2, jnp.where(b1, t[14], t[12]), jnp.where(b1, t[10], t[8]))
    lo = jnp.where(b2, jnp.where(b1, t[6], t[4]), jnp.where(b1, t[2], t[0]))
    b0 = test(jnp.where(b3, hi, lo))
    p = jnp.where(b3, 8.0, 0.0) + jnp.where(b2, 4.0, 0.0) + jnp.where(b1, 2.0, 0.0) + jnp.where(b0, 1.0, 0.0)
    return p + jnp.where(test(t[15]), 1.0, 0.0)


def _retrieve_kernel(xnt_ref, wqt_ref, keys_ref, r2_ref, e2_ref, m_ref, w_ref):
    heads = keys_ref.shape[0]
    dq = keys_ref.shape[3]
    k = PEER_TOPK
    tm = xnt_ref.shape[1]
    for h in range(heads):
        qt = jnp.dot(wqt_ref[2 * h * dq:(2 * h + 2) * dq, :], xnt_ref[...], preferred_element_type=F32)
        qt = qt.astype(BF16)
        s1_all = jnp.dot(keys_ref[h, 0], qt[:dq], preferred_element_type=F32)
        s2_all = jnp.dot(keys_ref[h, 1], qt[dq:], preferred_element_type=F32)
        for lo in range(0, tm, LANES):
            cols = slice(lo, lo + LANES)
            s1, s2 = s1_all[:, cols], s2_all[:, cols]
            top1, top2 = _top_sorted(s1), _top_sorted(s2)
            t1 = [t[0:1] for t in top1]
            t2 = [t[0:1] for t in top2]
            t2s = jnp.concatenate(t2, axis=0)
            pieces = [t1[i] + t2s[:k // (i + 1)] for i in range(k)]
            rows = sum(p.shape[0] for p in pieces)
            pieces.append(jnp.full((-rows % 8, LANES), -jnp.inf, F32))
            work = jnp.concatenate(pieces, axis=0)
            tau = None
            for _ in range(k):
                tau = jnp.max(work, axis=0, keepdims=True)
                work = jnp.where(work == tau, -jnp.inf, work)
            e2s = jnp.exp(t2s - t2[0])
            z = jnp.zeros_like(tau)
            for i in range(k):
                sel = (t1[i] + t2s) >= tau
                z = z + jnp.exp(t1[i] - t1[0]) * jnp.sum(jnp.where(sel, e2s, 0.0), axis=0, keepdims=True)
            rows8 = [slice(8 * i, 8 * (i + 1)) for i in range(N_KEYS // 8)]
            r2 = jnp.concatenate([_prefix_len(top2, lambda t, x=s2[r]: t > x) for r in rows8], axis=0)
            m = jnp.concatenate([_prefix_len(top2, lambda t, x=s1[r]: (x + t) >= tau) for r in rows8], axis=0)
            r2_ref[h, :, cols] = r2.astype(r2_ref.dtype)
            e2_ref[h, :, cols] = jnp.exp(s2 - t2[0]).astype(e2_ref.dtype)
            m_ref[h, :, cols] = m
            w_ref[h, :, cols] = (0.5 * jnp.exp(s1 - t1[0])) / z


def _retrieve(xnt, wqt, keys, tm):
    d, n = xnt.shape
    heads = keys.shape[0]
    assert tm % LANES == 0
    assert PEER_TOPK == 16 and keys.shape[2] == N_KEYS == 8 * PEER_TOPK
    big = lambda: pl.BlockSpec((heads, N_KEYS, tm), lambda i: (0, 0, i))
    shape = lambda dt: jax.ShapeDtypeStruct((heads, N_KEYS, n), dt)
    return pl.pallas_call(
        _retrieve_kernel,
        grid=(n // tm,),
        in_specs=[
            pl.BlockSpec((d, tm), lambda i: (0, i)),
            pl.BlockSpec(wqt.shape, lambda i: (0, 0)),
            pl.BlockSpec(keys.shape, lambda i: (0, 0, 0, 0)),
        ],
        out_specs=[big(), big(), big(), big()],
        out_shape=[shape(BF16), shape(BF16), shape(F32), shape(F32)],
        compiler_params=_params(("parallel",)),
        name="peer_retrieve",
    )(xnt, wqt, keys)


def _experts_kernel(n_blk, xnt_ref, u_ref, vt_ref, r2_ref, e2_ref, m_ref, w_ref, h_ref, g_ref, o_ref,
                    act0_ref, act1_ref, acc_ref):
    s = pl.program_id(0)
    heads = r2_ref.shape[0]
    ec = u_ref.shape[0]
    e_prev = jnp.maximum(s - 1, 0) % n_blk

    @pl.when(s == 0)
    def _():
        act1_ref[...] = jnp.zeros_like(act1_ref)

    @pl.when(e_prev == 0)
    def _():
        acc_ref[...] = jnp.zeros_like(acc_ref)

    def step(src_ref, dst_ref):
        c0 = e_prev * (ec // N_KEYS)
        mrows = [[m_ref[h, pl.ds(c0 + cc, 1), :].astype(BF16) for h in range(heads)]
                 for cc in range(ec // N_KEYS)]
        wrows = [[w_ref[h, pl.ds(c0 + cc, 1), :].astype(BF16) for h in range(heads)]
                 for cc in range(ec // N_KEYS)]
        for k0 in range(0, ec, EXPERT_GROUP):
            grp = slice(k0, k0 + EXPERT_GROUP)
            dst_ref[grp, :] = jnp.dot(u_ref[grp, :], xnt_ref[...], preferred_element_type=F32)
            ws = []
            for r0 in range(k0, k0 + EXPERT_GROUP, N_KEYS):
                gate = None
                for h in range(heads):
                    gh = jnp.where(r2_ref[h] < mrows[r0 // N_KEYS][h], e2_ref[h], 0) * wrows[r0 // N_KEYS][h]
                    gate = gh if gate is None else gate + gh
                a = src_ref[r0:r0 + N_KEYS, :]
                ge = a * (1.0 + lax.erf(a * (1.0 / math.sqrt(2.0))))
                ws.append(ge.astype(BF16) * gate)
            wg = jnp.concatenate(ws, axis=0)
            acc_ref[...] += jnp.dot(vt_ref[:, grp], wg, preferred_element_type=F32)

    @pl.when(s % 2 == 0)
    def _():
        step(act1_ref, act0_ref)

    @pl.when(s % 2 == 1)
    def _():
        step(act0_ref, act1_ref)

    @pl.when(jnp.logical_and(s > 0, e_prev == n_blk - 1))
    def _():
        hh = h_ref[...] + acc_ref[...].T
        ms = jnp.mean(hh * hh, axis=-1, keepdims=True)
        o_ref[...] = hh * lax.rsqrt(ms + EPS) * g_ref[...]


def _experts(xnt, u, vt, r2, e2, m, w, h2, g, tm, ec):
    d, n = xnt.shape
    n_exp = u.shape[0]
    heads = r2.shape[0]
    n_blk = n_exp // ec
    assert ec % EXPERT_GROUP == 0 and EXPERT_GROUP % N_KEYS == 0
    steps = (n // tm) * n_blk
    cur = lambda s: jnp.minimum(s, steps - 1)
    prev = lambda s: jnp.maximum(s - 1, 0)
    tile_spec = lambda shape, imap: pl.BlockSpec(shape, imap, pipeline_mode=pl.Buffered(1))
    big = lambda: tile_spec((heads, N_KEYS, tm), lambda s: (0, 0, prev(s) // n_blk))
    return pl.pallas_call(
        functools.partial(_experts_kernel, n_blk),
        grid=(steps + 1,),
        in_specs=[
            pl.BlockSpec((d, tm), lambda s: (0, cur(s) // n_blk)),
            pl.BlockSpec((ec, d), lambda s: (cur(s) % n_blk, 0)),
            pl.BlockSpec((d, ec), lambda s: (0, prev(s) % n_blk)),
            big(), big(), big(), big(),
            tile_spec((tm, d), lambda s: (prev(s) // n_blk, 0)),
            pl.BlockSpec((1, d), lambda s: (0, 0)),
        ],
        out_specs=pl.BlockSpec((tm, d), lambda s: (prev(s) // n_blk, 0)),
        out_shape=jax.ShapeDtypeStruct((n, d), F32),
        scratch_shapes=[pltpu.VMEM((ec, tm), F32), pltpu.VMEM((ec, tm), F32), pltpu.VMEM((d, tm), F32)],
        compiler_params=_params(("arbitrary",)),
        name="peer_experts",
    )(xnt, u, vt, r2, e2, m, w, h2, g)


def _transpose_cast_kernel(x_ref, o_ref):
    o_ref[...] = x_ref[...].T.astype(o_ref.dtype)


def _transpose_cast(x, tr):
    rows, cols = x.shape
    return pl.pallas_call(
        _transpose_cast_kernel,
        grid=(rows // tr,),
        in_specs=[pl.BlockSpec((tr, cols), lambda i: (i, 0))],
        out_specs=pl.BlockSpec((cols, tr), lambda i: (0, i)),
        out_shape=jax.ShapeDtypeStruct((cols, rows), BF16),
        compiler_params=_params(("parallel",)),
        name="transpose_cast",
    )(x)


def _tile(n, want):
    t = min(n, want)
    assert n % t == 0, (n, t)
    return t


def kernel(x, meta, ln1_g, w_in, conv_w, conv_b, rg_wa, rg_ba, rg_wx, rg_bx, rg_lambda, hg_lb_logits,
           hg_norm_g, w_pa, w_pb, w_out, ln2_g, peer_wq, peer_keys, peer_u, peer_v, final_g):
    batch, seq, d = x.shape
    n = batch * seq
    n_meta = meta.shape[0]
    depth = w_in.shape[0]
    assert depth == 1 and seq % TT == 0 and n_meta <= TT
    rg_width = conv_w.shape[2]
    hg_width = hg_norm_g.shape[1]
    assert rg_width == hg_width == d and rg_wa.shape[2] == HEAD
    n_pad = TT - n_meta
    row = lambda a: a.reshape(1, -1)

    xf = x.reshape(n, d)
    meta_tile = jnp.concatenate([jnp.zeros((n_pad, d), x.dtype), meta.astype(x.dtype)], axis=0)

    tm = _tile(n, 1024)
    proj = _norm_matmul(xf, row(ln1_g[0]), w_in[0], tm, 1024)
    proj_meta = _norm_matmul(meta_tile, row(ln1_g[0]), w_in[0], TT, 1024)

    y_a = _rglru(proj, proj_meta, batch, seq, n_pad, conv_w[0], row(conv_b[0]), rg_wa[0].astype(BF16),
                 row(rg_ba[0]), rg_wx[0].astype(BF16), row(rg_bx[0]), row(rg_lambda[0]))
    y_b = _hgrn(proj, proj_meta, batch, seq, n_pad, 2, hg_lb_logits, row(hg_norm_g[0]))

    tn = 1024
    mixed = _merge(y_a, y_b, w_pa[0].astype(BF16), w_pb[0].astype(BF16), proj, 6 * (d // tn), tm, tn)
    h2, xn2t = _outproj(mixed, w_out[0].astype(BF16), xf, row(ln2_g[0]), _tile(n, 512))

    tp = _tile(n, 512)
    r2, e2, m, w = _retrieve(xn2t, peer_wq[0].T.astype(BF16), peer_keys[0].astype(BF16), tp)
    out = _experts(xn2t, peer_u[0].astype(BF16), _transpose_cast(peer_v[0], 512), r2, e2, m, w, h2,
                   row(final_g), tp, 1024)
    return out.reshape(batch, seq, d)
```

```python
import functools
import math

import jax
import jax.numpy as jnp
from jax import lax
from jax.experimental import pallas as pl
from jax.experimental.pallas import tpu as pltpu

F32 = jnp.float32
BF16 = jnp.bfloat16

EPS = 1e-6
RG_C = 8.0
CONV_WIDTH = 4
HEAD = 128
N_KEYS = 128
PEER_TOPK = 16
LANES = 128
SUBLANES = 8
VMEM_LIMIT = 56 * 1024 * 1024

TT = 128
RB = 32
PROJ_ROWS = 1024
PROJ_COLS = 1024
OUT_ROWS = 512
PEER_TOKENS = 512
PEER_EXPERTS = 1024
CAST_ROWS = 512

NT_DIMS = (((1,), (1,)), ((), ()))
TN_DIMS = (((0,), (0,)), ((), ()))


def _params(sem):
    return pltpu.CompilerParams(dimension_semantics=sem, vmem_limit_bytes=VMEM_LIMIT)


def _sigmoid(x):
    return jax.nn.sigmoid(x)


def _gelu(x):
    return 0.5 * x * (1.0 + lax.erf(x * (1.0 / math.sqrt(2.0))))


def _norm_mm_kernel(x_ref, g_ref, w_ref, o_ref, xn_ref):
    @pl.when(pl.program_id(1) == 0)
    def _():
        xf = x_ref[...]
        ms = jnp.mean(xf * xf, axis=-1, keepdims=True)
        xn_ref[...] = (xf * lax.rsqrt(ms + EPS) * g_ref[...]).astype(BF16)

    o_ref[...] = jnp.dot(xn_ref[...], w_ref[...], preferred_element_type=F32).astype(o_ref.dtype)


def _norm_matmul(x, g, w, tm, tn):
    m, d = x.shape
    n = w.shape[1]
    return pl.pallas_call(
        _norm_mm_kernel,
        grid=(m // tm, n // tn),
        in_specs=[
            pl.BlockSpec((tm, d), lambda i, j: (i, 0)),
            pl.BlockSpec((1, d), lambda i, j: (0, 0)),
            pl.BlockSpec((d, tn), lambda i, j: (0, j)),
        ],
        out_specs=pl.BlockSpec((tm, tn), lambda i, j: (i, j)),
        out_shape=jax.ShapeDtypeStruct((m, n), BF16),
        scratch_shapes=[pltpu.VMEM((tm, d), BF16)],
        compiler_params=_params(("parallel", "arbitrary")),
        name="norm_matmul",
    )(x, g, w)


def _rglru_kernel(n_pad, xa_ref, ya_ref, xam_ref, yam_ref, cw_ref, cb_ref, wa_ref, ba_ref, wx_ref, bx_ref,
                  lam_ref, o_ref, xbuf_ref, hc_ref):
    t = pl.program_id(1)
    width = xa_ref.shape[1]
    is_meta = t == 0
    halo = SUBLANES

    @pl.when(is_meta)
    def _():
        xbuf_ref[0:halo, :] = jnp.zeros((halo, width), F32)
        hc_ref[...] = jnp.zeros_like(hc_ref)

    xbuf_ref[halo:halo + TT, :] = jnp.where(is_meta, xam_ref[...], xa_ref[...]).astype(F32)
    row = lax.broadcasted_iota(jnp.int32, (TT, HEAD), 0)
    live = jnp.logical_or(t > 0, row >= n_pad)
    r8 = row & (SUBLANES - 1)

    for h in range(width // HEAD):
        sl = slice(h * HEAD, (h + 1) * HEAD)
        x0 = halo - (CONV_WIDTH - 1)
        xc = cb_ref[:, sl] + cw_ref[0:1, sl] * xbuf_ref[x0:x0 + TT, sl]
        for k in range(1, CONV_WIDTH):
            xc = xc + cw_ref[k:k + 1, sl] * xbuf_ref[x0 + k:x0 + k + TT, sl]
        xcb = xc.astype(BF16)
        r = _sigmoid(jnp.dot(xcb, wa_ref[h], preferred_element_type=F32) + ba_ref[:, sl])
        i = _sigmoid(jnp.dot(xcb, wx_ref[h], preferred_element_type=F32) + bx_ref[:, sl])
        sp = jax.nn.softplus(-lam_ref[:, sl])
        log_a = (-RG_C) * r * sp
        a = jnp.exp(log_a)
        th = jnp.tanh(log_a)
        u = jnp.sqrt(-2.0 * th / (1.0 - th)) * (i * xc)
        u = jnp.where(live, u, 0.0)
        for k in (1, 2, 4):
            a_sh = pltpu.roll(a, k, 0)
            u_sh = pltpu.roll(u, k, 0)
            m = r8 >= k
            u = jnp.where(m, a * u_sh + u, u)
            a = jnp.where(m, a * a_sh, a)
        carry = hc_ref[:, sl]
        outs = []
        for g in range(TT // SUBLANES):
            rows = slice(g * SUBLANES, (g + 1) * SUBLANES)
            hg = a[rows] * carry + u[rows]
            outs.append(hg)
            carry = hg[SUBLANES - 1:SUBLANES]
        hc_ref[:, sl] = carry
        hs = jnp.concatenate(outs, axis=0)
        ya = jnp.where(is_meta, yam_ref[:, sl], ya_ref[:, sl]).astype(F32)
        o_ref[:, sl] = (_gelu(ya) * hs).astype(o_ref.dtype)

    xbuf_ref[0:halo, :] = xbuf_ref[TT:TT + halo, :]


def _rglru(proj, proj_meta, batch, seq, n_pad, conv_w, conv_b, wa, ba, wx, bx, lam):
    n = batch * seq
    width = conv_w.shape[1]
    tiles = seq // TT
    in_row = lambda b, t: b * tiles + jnp.maximum(t - 1, 0)

    vec = lambda: pl.BlockSpec((1, width), lambda b, t: (0, 0))
    gate_w = lambda: pl.BlockSpec((width // HEAD, HEAD, HEAD), lambda b, t: (0, 0, 0))
    return pl.pallas_call(
        functools.partial(_rglru_kernel, n_pad),
        grid=(batch, tiles + 1),
        in_specs=[
            pl.BlockSpec((TT, width), lambda b, t: (in_row(b, t), 0)),
            pl.BlockSpec((TT, width), lambda b, t: (in_row(b, t), 1)),
            pl.BlockSpec((TT, width), lambda b, t: (0, 0)),
            pl.BlockSpec((TT, width), lambda b, t: (0, 1)),
            pl.BlockSpec((CONV_WIDTH, width), lambda b, t: (0, 0)),
            vec(), gate_w(), vec(), gate_w(), vec(), vec(),
        ],
        out_specs=pl.BlockSpec((TT, width), lambda b, t: (in_row(b, t), 0)),
        out_shape=jax.ShapeDtypeStruct((n, width), BF16),
        scratch_shapes=[pltpu.VMEM((TT + SUBLANES, width), F32), pltpu.VMEM((1, width), F32)],
        compiler_params=_params(("parallel", "arbitrary")),
        name="rglru",
    )(proj, proj, proj_meta, proj_meta, conv_w, conv_b, wa, ba, wx, bx, lam)


def _hgrn_kernel(n_pad, q_ref, f_ref, v_ref, g_ref, qm_ref, fm_ref, vm_ref, gm_ref, lbl_ref, ng_ref, o_ref,
                 st_ref):
    t = pl.program_id(1)
    width = q_ref.shape[1]
    is_meta = t == 0

    @pl.when(is_meta)
    def _():
        st_ref[...] = jnp.zeros_like(st_ref)

    pick = lambda meta_ref, ref: jnp.where(is_meta, meta_ref[...], ref[...])
    q_in, f_in, v_in, g_in = pick(qm_ref, q_ref), pick(fm_ref, f_ref), pick(vm_ref, v_ref), pick(gm_ref, g_ref)

    heads = [slice(h * HEAD, (h + 1) * HEAD) for h in range(width // HEAD)]
    blocks = [(r0, r0 + RB, r0 + RB // 2 - 1) for r0 in range(0, TT, RB)]

    row = lax.broadcasted_iota(jnp.int32, (TT, width), 0)
    live = jnp.logical_or(t > 0, row >= n_pad)
    lg = lbl_ref[...]
    e = jnp.exp(lg - jnp.max(lg, axis=0, keepdims=True))
    lb = e[0:1] / jnp.sum(e, axis=0, keepdims=True)
    f = lb + (1.0 - lb) * _sigmoid(f_in.astype(F32))
    kk = 1.0 - f
    bcum = jnp.where(live, jnp.log(f), 0.0)
    k = 1
    while k < TT:
        bcum = bcum + jnp.where(row >= k, pltpu.roll(bcum, k, 0), 0.0)
        k *= 2
    bend = bcum[TT - 1:TT]
    q = q_in.astype(F32)
    qs = q * _sigmoid(q)
    qhat = (qs * jnp.exp(bcum)).astype(BF16)
    khat = (kk * jnp.exp(bend - bcum)).astype(BF16)
    dec = jnp.exp(bend)
    qts, kts = [], []
    for r0, r1, mid in blocks:
        bref = bcum[mid:mid + 1]
        qts.append((qs[r0:r1] * jnp.exp(bcum[r0:r1] - bref)).astype(BF16))
        kt = kk * jnp.exp(bref - bcum)
        kts.append((kt if r1 == TT else jnp.where(row < r1, kt, 0.0)).astype(BF16))

    tri = lax.broadcasted_iota(jnp.int32, (TT, TT), 0) >= lax.broadcasted_iota(jnp.int32, (TT, TT), 1)
    o_inter = [lax.dot_general(qhat[:, sl], st_ref[h].astype(BF16), NT_DIMS, preferred_element_type=F32)
               for h, sl in enumerate(heads)]
    scores = [[lax.dot_general(qt[:, sl], kt[:, sl], NT_DIMS, preferred_element_type=F32)
               for qt, kt in zip(qts, kts)] for sl in heads]
    outs = []
    for h, sl in enumerate(heads):
        p = jnp.where(tri, jnp.concatenate(scores[h], axis=0), 0.0).astype(BF16)
        outs.append(jnp.dot(p, v_in[:, sl], preferred_element_type=F32) + o_inter[h])
    for h, sl in enumerate(heads):
        st_ref[h] = st_ref[h] * dec[:, sl] + lax.dot_general(v_in[:, sl], khat[:, sl], TN_DIMS,
                                                             preferred_element_type=F32)
    for h, sl in enumerate(heads):
        o = outs[h]
        o = o * lax.rsqrt(jnp.mean(o * o, axis=-1, keepdims=True) + EPS)
        og = g_in[:, sl].astype(F32)
        o_ref[:, sl] = (o * ng_ref[:, sl] * (og * _sigmoid(og))).astype(o_ref.dtype)


def _hgrn(proj, proj_meta, batch, seq, n_pad, col0, lb_logits, norm_g):
    n = batch * seq
    width = norm_g.shape[1]
    tiles = seq // TT
    in_row = lambda b, t: b * tiles + jnp.maximum(t - 1, 0)

    col = lambda c: pl.BlockSpec((TT, width), lambda b, t: (in_row(b, t), col0 + c))
    meta_col = lambda c: pl.BlockSpec((TT, width), lambda b, t: (0, col0 + c))
    return pl.pallas_call(
        functools.partial(_hgrn_kernel, n_pad),
        grid=(batch, tiles + 1),
        in_specs=[
            col(0), col(1), col(2), col(3),
            meta_col(0), meta_col(1), meta_col(2), meta_col(3),
            pl.BlockSpec(lb_logits.shape, lambda b, t: (0, 0)),
            pl.BlockSpec((1, width), lambda b, t: (0, 0)),
        ],
        out_specs=pl.BlockSpec((TT, width), lambda b, t: (in_row(b, t), 0)),
        out_shape=jax.ShapeDtypeStruct((n, width), BF16),
        scratch_shapes=[pltpu.VMEM((width // HEAD, HEAD, HEAD), F32)],
        compiler_params=_params(("parallel", "arbitrary")),
        name="hgrn2",
    )(proj, proj, proj, proj, proj_meta, proj_meta, proj_meta, proj_meta, lb_logits, norm_g)


def _merge_kernel(ya_ref, yb_ref, wpa_ref, wpb_ref, za_ref, zb_ref, o_ref):
    ta = jnp.dot(ya_ref[...], wpa_ref[...], preferred_element_type=F32)
    tb = jnp.dot(yb_ref[...], wpb_ref[...], preferred_element_type=F32)
    mixed = _sigmoid(za_ref[...].astype(F32)) * ta + _sigmoid(zb_ref[...].astype(F32)) * tb
    o_ref[...] = mixed.astype(o_ref.dtype)


def _merge(ya, yb, wpa, wpb, proj, zcol0, tm, tn):
    n, d = ya.shape
    dm = wpa.shape[1]
    nj = dm // tn
    return pl.pallas_call(
        _merge_kernel,
        grid=(n // tm, nj),
        in_specs=[
            pl.BlockSpec((tm, d), lambda i, j: (i, 0)),
            pl.BlockSpec((tm, d), lambda i, j: (i, 0)),
            pl.BlockSpec((d, tn), lambda i, j: (0, j)),
            pl.BlockSpec((d, tn), lambda i, j: (0, j)),
            pl.BlockSpec((tm, tn), lambda i, j: (i, zcol0 + j)),
            pl.BlockSpec((tm, tn), lambda i, j: (i, zcol0 + nj + j)),
        ],
        out_specs=pl.BlockSpec((tm, tn), lambda i, j: (i, j)),
        out_shape=jax.ShapeDtypeStruct((n, dm), BF16),
        compiler_params=_params(("parallel", "arbitrary")),
        name="merge",
    )(ya, yb, wpa, wpb, proj, proj)


def _outproj_kernel(m_ref, w_ref, x_ref, g_ref, h_ref, xnt_ref):
    h = x_ref[...] + jnp.dot(m_ref[...], w_ref[...], preferred_element_type=F32)
    h_ref[...] = h
    ms = jnp.mean(h * h, axis=-1, keepdims=True)
    xn = h * lax.rsqrt(ms + EPS) * g_ref[...]
    xnt_ref[...] = xn.T.astype(xnt_ref.dtype)


def _outproj(mixed, w_out, x, g, tm):
    n, d = x.shape
    return pl.pallas_call(
        _outproj_kernel,
        grid=(n // tm,),
        in_specs=[
            pl.BlockSpec((tm, d), lambda i: (i, 0)),
            pl.BlockSpec((d, d), lambda i: (0, 0)),
            pl.BlockSpec((tm, d), lambda i: (i, 0)),
            pl.BlockSpec((1, d), lambda i: (0, 0)),
        ],
        out_specs=[pl.BlockSpec((tm, d), lambda i: (i, 0)), pl.BlockSpec((d, tm), lambda i: (0, i))],
        out_shape=[jax.ShapeDtypeStruct((n, d), F32), jax.ShapeDtypeStruct((d, n), BF16)],
        compiler_params=_params(("parallel",)),
        name="outproj",
    )(mixed, w_out, x, g)


def _odd_even_merge_sort_pairs(n):
    pairs = []

    def merge(lo, cnt, r):
        step = r * 2
        if step < cnt:
            merge(lo, cnt, step)
            merge(lo + r, cnt, step)
            pairs.extend((i, i + r) for i in range(lo + r, lo + cnt - r, step))
        else:
            pairs.append((lo, lo + r))

    def sort(lo, cnt):
        if cnt > 1:
            sort(lo, cnt // 2)
            sort(lo + cnt // 2, cnt // 2)
            merge(lo, cnt, 1)

    sort(0, n)
    return pairs


def _order(v, i, j):
    v[i], v[j] = jnp.maximum(v[i], v[j]), jnp.minimum(v[i], v[j])


def _top_sorted(s):
    n = s.shape[0] // SUBLANES
    v = [s[SUBLANES * i:SUBLANES * (i + 1)] for i in range(n)]
    for i, j in _odd_even_merge_sort_pairs(n):
        _order(v, i, j)
    shift = SUBLANES // 2
    while shift:
        v = [jnp.maximum(v[i], pltpu.roll(v[n - 1 - i], shift, 0)) for i in range(n)]
        d = n // 2
        while d:
            for i in range(n):
                if not i & d:
                    _order(v, i, i + d)
            d //= 2
        shift //= 2
    return v


def _prefix_len(t, test):
    b3 = test(t[7])
    b2 = test(jnp.where(b3, t[11], t[3]))
    b1 = test(jnp.where(b3, jnp.where(b2, t[13], t[9]), jnp.where(b2, t[5], t[1])))
    hi = jnp.where(b2, jnp.where(b1, t[14], t[12]), jnp.where(b1, t[10], t[8]))
    lo = jnp.where(b2, jnp.where(b1, t[6], t[4]), jnp.where(b1, t[2], t[0]))
    b0 = test(jnp.where(b3, hi, lo))
    p = jnp.where(b3, 8.0, 0.0) + jnp.where(b2, 4.0, 0.0) + jnp.where(b1, 2.0, 0.0) + jnp.where(b0, 1.0, 0.0)
    return p + jnp.where(test(t[15]), 1.0, 0.0)


def _retrieve_kernel(xnt_ref, wqt_ref, keys_ref, r2_ref, e2_ref, m_ref, w_ref):
    heads = keys_ref.shape[0]
    dq = keys_ref.shape[3]
    k = PEER_TOPK
    tm = xnt_ref.shape[1]
    for h in range(heads):
        qt = jnp.dot(wqt_ref[2 * h * dq:(2 * h + 2) * dq, :], xnt_ref[...], preferred_element_type=F32)
        qt = qt.astype(BF16)
        s1_all = jnp.dot(keys_ref[h, 0], qt[:dq], preferred_element_type=F32)
        s2_all = jnp.dot(keys_ref[h, 1], qt[dq:], preferred_element_type=F32)
        for lo in range(0, tm, LANES):
            cols = slice(lo, lo + LANES)
            s1, s2 = s1_all[:, cols], s2_all[:, cols]
            top1, top2 = _top_sorted(s1), _top_sorted(s2)
            t1 = [t[0:1] for t in top1]
            t2 = [t[0:1] for t in top2]
            t2s = jnp.concatenate(t2, axis=0)
            pieces = [t1[i] + t2s[:k // (i + 1)] for i in range(k)]
            rows = sum(p.shape[0] for p in pieces)
            pieces.append(jnp.full((-rows % SUBLANES, LANES), -jnp.inf, F32))
            work = jnp.concatenate(pieces, axis=0)
            tau = None
            for _ in range(k):
                tau = jnp.max(work, axis=0, keepdims=True)
                work = jnp.where(work == tau, -jnp.inf, work)
            e2s = jnp.exp(t2s - t2[0])
            z = jnp.zeros_like(tau)
            for i in range(k):
                sel = (t1[i] + t2s) >= tau
                z = z + jnp.exp(t1[i] - t1[0]) * jnp.sum(jnp.where(sel, e2s, 0.0), axis=0, keepdims=True)
            rows8 = [slice(SUBLANES * i, SUBLANES * (i + 1)) for i in range(N_KEYS // SUBLANES)]
            r2 = jnp.concatenate([_prefix_len(top2, lambda t, x=s2[r]: t > x) for r in rows8], axis=0)
            m = jnp.concatenate([_prefix_len(top2, lambda t, x=s1[r]: (x + t) >= tau) for r in rows8], axis=0)
            r2_ref[h, :, cols] = r2.astype(r2_ref.dtype)
            e2_ref[h, :, cols] = jnp.exp(s2 - t2[0]).astype(e2_ref.dtype)
            m_ref[h, :, cols] = m
            w_ref[h, :, cols] = (0.5 * jnp.exp(s1 - t1[0])) / z


def _retrieve(xnt, wqt, keys, tm):
    d, n = xnt.shape
    heads = keys.shape[0]
    assert tm % LANES == 0
    assert PEER_TOPK == 16 and keys.shape[2] == N_KEYS == SUBLANES * PEER_TOPK
    big = lambda: pl.BlockSpec((heads, N_KEYS, tm), lambda i: (0, 0, i))
    shape = lambda dt: jax.ShapeDtypeStruct((heads, N_KEYS, n), dt)
    return pl.pallas_call(
        _retrieve_kernel,
        grid=(n // tm,),
        in_specs=[
            pl.BlockSpec((d, tm), lambda i: (0, i)),
            pl.BlockSpec(wqt.shape, lambda i: (0, 0)),
            pl.BlockSpec(keys.shape, lambda i: (0, 0, 0, 0)),
        ],
        out_specs=[big(), big(), big(), big()],
        out_shape=[shape(BF16), shape(BF16), shape(F32), shape(F32)],
        compiler_params=_params(("parallel",)),
        name="peer_retrieve",
    )(xnt, wqt, keys)


def _experts_kernel(n_blk, xnt_ref, u_ref, vt_ref, r2_ref, e2_ref, m_ref, w_ref, h_ref, g_ref, o_ref,
                    act0_ref, act1_ref, acc_ref):
    s = pl.program_id(0)
    heads = r2_ref.shape[0]
    ec = u_ref.shape[0]
    chunks = ec // N_KEYS
    e_prev = jnp.maximum(s - 1, 0) % n_blk

    @pl.when(s == 0)
    def _():
        act1_ref[...] = jnp.zeros_like(act1_ref)

    @pl.when(e_prev == 0)
    def _():
        acc_ref[...] = jnp.zeros_like(acc_ref)

    def step(src_ref, dst_ref):
        c0 = e_prev * chunks
        mrows = [[m_ref[h, pl.ds(c0 + cc, 1), :].astype(BF16) for h in range(heads)] for cc in range(chunks)]
        wrows = [[w_ref[h, pl.ds(c0 + cc, 1), :].astype(BF16) for h in range(heads)] for cc in range(chunks)]
        dst_ref[...] = jnp.dot(u_ref[...], xnt_ref[...], preferred_element_type=F32)
        ws = []
        for cc in range(chunks):
            gate = None
            for h in range(heads):
                gh = jnp.where(r2_ref[h] < mrows[cc][h], e2_ref[h], 0) * wrows[cc][h]
                gate = gh if gate is None else gate + gh
            a = src_ref[cc * N_KEYS:(cc + 1) * N_KEYS, :]
            ge = a * (1.0 + lax.erf(a * (1.0 / math.sqrt(2.0))))
            ws.append(ge.astype(BF16) * gate)
        wg = jnp.concatenate(ws, axis=0)
        acc_ref[...] += jnp.dot(vt_ref[...], wg, preferred_element_type=F32)

    @pl.when(s % 2 == 0)
    def _():
        step(act1_ref, act0_ref)

    @pl.when(s % 2 == 1)
    def _():
        step(act0_ref, act1_ref)

    @pl.when(jnp.logical_and(s > 0, e_prev == n_blk - 1))
    def _():
        hh = h_ref[...] + acc_ref[...].T
        ms = jnp.mean(hh * hh, axis=-1, keepdims=True)
        o_ref[...] = hh * lax.rsqrt(ms + EPS) * g_ref[...]


def _experts(xnt, u, vt, r2, e2, m, w, h2, g, tm, ec):
    d, n = xnt.shape
    n_exp = u.shape[0]
    heads = r2.shape[0]
    n_blk = n_exp // ec
    assert ec % N_KEYS == 0
    steps = (n // tm) * n_blk
    cur = lambda s: jnp.minimum(s, steps - 1)
    prev = lambda s: jnp.maximum(s - 1, 0)
    tile_spec = lambda shape, imap: pl.BlockSpec(shape, imap, pipeline_mode=pl.Buffered(1))
    big = lambda: tile_spec((heads, N_KEYS, tm), lambda s: (0, 0, prev(s) // n_blk))
    return pl.pallas_call(
        functools.partial(_experts_kernel, n_blk),
        grid=(steps + 1,),
        in_specs=[
            pl.BlockSpec((d, tm), lambda s: (0, cur(s) // n_blk)),
            pl.BlockSpec((ec, d), lambda s: (cur(s) % n_blk, 0)),
            pl.BlockSpec((d, ec), lambda s: (0, prev(s) % n_blk)),
            big(), big(), big(), big(),
            tile_spec((tm, d), lambda s: (prev(s) // n_blk, 0)),
            pl.BlockSpec((1, d), lambda s: (0, 0)),
        ],
        out_specs=pl.BlockSpec((tm, d), lambda s: (prev(s) // n_blk, 0)),
        out_shape=jax.ShapeDtypeStruct((n, d), F32),
        scratch_shapes=[pltpu.VMEM((ec, tm), F32), pltpu.VMEM((ec, tm), F32), pltpu.VMEM((d, tm), F32)],
        compiler_params=_params(("arbitrary",)),
        name="peer_experts",
    )(xnt, u, vt, r2, e2, m, w, h2, g)


def _transpose_cast_kernel(x_ref, o_ref):
    o_ref[...] = x_ref[...].T.astype(o_ref.dtype)


def _transpose_cast(x, tr):
    rows, cols = x.shape
    return pl.pallas_call(
        _transpose_cast_kernel,
        grid=(rows // tr,),
        in_specs=[pl.BlockSpec((tr, cols), lambda i: (i, 0))],
        out_specs=pl.BlockSpec((cols, tr), lambda i: (0, i)),
        out_shape=jax.ShapeDtypeStruct((cols, rows), BF16),
        compiler_params=_params(("parallel",)),
        name="transpose_cast",
    )(x)


def _tile(n, want):
    t = min(n, want)
    assert n % t == 0, (n, t)
    return t


def kernel(x, meta, ln1_g, w_in, conv_w, conv_b, rg_wa, rg_ba, rg_wx, rg_bx, rg_lambda, hg_lb_logits,
           hg_norm_g, w_pa, w_pb, w_out, ln2_g, peer_wq, peer_keys, peer_u, peer_v, final_g):
    batch, seq, d = x.shape
    n = batch * seq
    n_meta = meta.shape[0]
    depth = w_in.shape[0]
    assert depth == 1 and seq % TT == 0 and n_meta <= TT
    rg_width = conv_w.shape[2]
    hg_width = hg_norm_g.shape[1]
    assert rg_width == hg_width == d and rg_wa.shape[2] == HEAD and d % PROJ_COLS == 0
    n_pad = TT - n_meta
    row = lambda a: a.reshape(1, -1)

    xf = x.reshape(n, d)
    meta_tile = jnp.concatenate([jnp.zeros((n_pad, d), x.dtype), meta.astype(x.dtype)], axis=0)

    w_in_b = w_in[0].astype(BF16)
    tm = _tile(n, PROJ_ROWS)
    proj = _norm_matmul(xf, row(ln1_g[0]), w_in_b, tm, PROJ_COLS)
    proj_meta = _norm_matmul(meta_tile, row(ln1_g[0]), w_in_b, TT, PROJ_COLS)

    y_a = _rglru(proj, proj_meta, batch, seq, n_pad, conv_w[0], row(conv_b[0]), rg_wa[0].astype(BF16),
                 row(rg_ba[0]), rg_wx[0].astype(BF16), row(rg_bx[0]), row(rg_lambda[0]))
    y_b = _hgrn(proj, proj_meta, batch, seq, n_pad, 2, hg_lb_logits, row(hg_norm_g[0]))

    mixed = _merge(y_a, y_b, w_pa[0].astype(BF16), w_pb[0].astype(BF16), proj, 6 * (d // PROJ_COLS), tm,
                   PROJ_COLS)
    h2, xn2t = _outproj(mixed, w_out[0].astype(BF16), xf, row(ln2_g[0]), _tile(n, OUT_ROWS))

    tp = _tile(n, PEER_TOKENS)
    r2, e2, m, w = _retrieve(xn2t, peer_wq[0].T.astype(BF16), peer_keys[0].astype(BF16), tp)
    out = _experts(xn2t, peer_u[0].astype(BF16), _transpose_cast(peer_v[0], CAST_ROWS), r2, e2, m, w, h2,
                   row(final_g), tp, PEER_EXPERTS)
    return out.reshape(batch, seq, d)
```

```python
import functools
import math

import jax
import jax.numpy as jnp
from jax import lax
from jax.experimental import pallas as pl
from jax.experimental.pallas import tpu as pltpu

F32 = jnp.float32
BF16 = jnp.bfloat16

EPS = 1e-6
RG_C = 8.0
CONV_WIDTH = 4
HEAD = 128
N_KEYS = 128
PEER_TOPK = 16
LANES = 128
SUBLANES = 8
VMEM_LIMIT = 56 * 1024 * 1024

TT = 128
RB = 32
PROJ_ROWS = 1024
PROJ_COLS = 1024
OUT_ROWS = 512
PEER_TOKENS = 512
PEER_EXPERTS = 1024
CAST_ROWS = 512

NT_DIMS = (((1,), (1,)), ((), ()))
TN_DIMS = (((0,), (0,)), ((), ()))


def _params(sem):
    return pltpu.CompilerParams(dimension_semantics=sem, vmem_limit_bytes=VMEM_LIMIT)


def _sigmoid(x):
    return jax.nn.sigmoid(x)


def _gelu(x):
    return 0.5 * x * (1.0 + lax.erf(x * (1.0 / math.sqrt(2.0))))


def _norm_mm_kernel(x_ref, g_ref, w_ref, o_ref, xn_ref):
    @pl.when(pl.program_id(1) == 0)
    def _():
        xf = x_ref[...]
        ms = jnp.mean(xf * xf, axis=-1, keepdims=True)
        xn_ref[...] = (xf * lax.rsqrt(ms + EPS) * g_ref[...]).astype(BF16)

    o_ref[...] = jnp.dot(xn_ref[...], w_ref[...], preferred_element_type=F32).astype(o_ref.dtype)


def _norm_matmul(x, g, w, tm, tn):
    m, d = x.shape
    n = w.shape[1]
    return pl.pallas_call(
        _norm_mm_kernel,
        grid=(m // tm, n // tn),
        in_specs=[
            pl.BlockSpec((tm, d), lambda i, j: (i, 0)),
            pl.BlockSpec((1, d), lambda i, j: (0, 0)),
            pl.BlockSpec((d, tn), lambda i, j: (0, j)),
        ],
        out_specs=pl.BlockSpec((tm, tn), lambda i, j: (i, j)),
        out_shape=jax.ShapeDtypeStruct((m, n), BF16),
        scratch_shapes=[pltpu.VMEM((tm, d), BF16)],
        compiler_params=_params(("parallel", "arbitrary")),
        name="norm_matmul",
    )(x, g, w)


def _rglru_kernel(n_pad, xa_ref, ya_ref, xam_ref, yam_ref, cw_ref, cb_ref, wa_ref, ba_ref, wx_ref, bx_ref,
                  lam_ref, o_ref, xbuf_ref, hc_ref):
    t = pl.program_id(1)
    width = xa_ref.shape[1]
    is_meta = t == 0
    halo = SUBLANES

    @pl.when(is_meta)
    def _():
        xbuf_ref[0:halo, :] = jnp.zeros((halo, width), F32)
        hc_ref[...] = jnp.zeros_like(hc_ref)

    xbuf_ref[halo:halo + TT, :] = jnp.where(is_meta, xam_ref[...], xa_ref[...]).astype(F32)
    row = lax.broadcasted_iota(jnp.int32, (TT, HEAD), 0)
    live = jnp.logical_or(t > 0, row >= n_pad)
    r8 = row & (SUBLANES - 1)

    for h in range(width // HEAD):
        sl = slice(h * HEAD, (h + 1) * HEAD)
        x0 = halo - (CONV_WIDTH - 1)
        xc = cb_ref[:, sl] + cw_ref[0:1, sl] * xbuf_ref[x0:x0 + TT, sl]
        for k in range(1, CONV_WIDTH):
            xc = xc + cw_ref[k:k + 1, sl] * xbuf_ref[x0 + k:x0 + k + TT, sl]
        xcb = xc.astype(BF16)
        r = _sigmoid(jnp.dot(xcb, wa_ref[h], preferred_element_type=F32) + ba_ref[:, sl])
        i = _sigmoid(jnp.dot(xcb, wx_ref[h], preferred_element_type=F32) + bx_ref[:, sl])
        sp = jax.nn.softplus(-lam_ref[:, sl])
        log_a = (-RG_C) * r * sp
        a = jnp.exp(log_a)
        th = jnp.tanh(log_a)
        u = jnp.sqrt(-2.0 * th / (1.0 - th)) * (i * xc)
        u = jnp.where(live, u, 0.0)
        for k in (1, 2, 4):
            a_sh = pltpu.roll(a, k, 0)
            u_sh = pltpu.roll(u, k, 0)
            m = r8 >= k
            u = jnp.where(m, a * u_sh + u, u)
            a = jnp.where(m, a * a_sh, a)
        carry = hc_ref[:, sl]
        outs = []
        for g in range(TT // SUBLANES):
            rows = slice(g * SUBLANES, (g + 1) * SUBLANES)
            hg = a[rows] * carry + u[rows]
            outs.append(hg)
            carry = hg[SUBLANES - 1:SUBLANES]
        hc_ref[:, sl] = carry
        hs = jnp.concatenate(outs, axis=0)
        ya = jnp.where(is_meta, yam_ref[:, sl], ya_ref[:, sl]).astype(F32)
        o_ref[:, sl] = (_gelu(ya) * hs).astype(o_ref.dtype)

    xbuf_ref[0:halo, :] = xbuf_ref[TT:TT + halo, :]


def _rglru(proj, proj_meta, batch, seq, n_pad, conv_w, conv_b, wa, ba, wx, bx, lam):
    n = batch * seq
    width = conv_w.shape[1]
    tiles = seq // TT
    in_row = lambda b, t: b * tiles + jnp.maximum(t - 1, 0)

    vec = lambda: pl.BlockSpec((1, width), lambda b, t: (0, 0))
    gate_w = lambda: pl.BlockSpec((width // HEAD, HEAD, HEAD), lambda b, t: (0, 0, 0))
    return pl.pallas_call(
        functools.partial(_rglru_kernel, n_pad),
        grid=(batch, tiles + 1),
        in_specs=[
            pl.BlockSpec((TT, width), lambda b, t: (in_row(b, t), 0)),
            pl.BlockSpec((TT, width), lambda b, t: (in_row(b, t), 1)),
            pl.BlockSpec((TT, width), lambda b, t: (0, 0)),
            pl.BlockSpec((TT, width), lambda b, t: (0, 1)),
            pl.BlockSpec((CONV_WIDTH, width), lambda b, t: (0, 0)),
            vec(), gate_w(), vec(), gate_w(), vec(), vec(),
        ],
        out_specs=pl.BlockSpec((TT, width), lambda b, t: (in_row(b, t), 0)),
        out_shape=jax.ShapeDtypeStruct((n, width), BF16),
        scratch_shapes=[pltpu.VMEM((TT + SUBLANES, width), F32), pltpu.VMEM((1, width), F32)],
        compiler_params=_params(("parallel", "arbitrary")),
        name="rglru",
    )(proj, proj, proj_meta, proj_meta, conv_w, conv_b, wa, ba, wx, bx, lam)


def _hgrn_kernel(n_pad, q_ref, f_ref, v_ref, g_ref, qm_ref, fm_ref, vm_ref, gm_ref, lbl_ref, ng_ref, o_ref,
                 st_ref):
    t = pl.program_id(1)
    width = q_ref.shape[1]
    is_meta = t == 0

    @pl.when(is_meta)
    def _():
        st_ref[...] = jnp.zeros_like(st_ref)

    pick = lambda meta_ref, ref: jnp.where(is_meta, meta_ref[...], ref[...])
    q_in, f_in, v_in, g_in = pick(qm_ref, q_ref), pick(fm_ref, f_ref), pick(vm_ref, v_ref), pick(gm_ref, g_ref)

    heads = [slice(h * HEAD, (h + 1) * HEAD) for h in range(width // HEAD)]
    blocks = [(r0, r0 + RB, r0 + RB // 2 - 1) for r0 in range(0, TT, RB)]

    row = lax.broadcasted_iota(jnp.int32, (TT, width), 0)
    live = jnp.logical_or(t > 0, row >= n_pad)
    lg = lbl_ref[...]
    e = jnp.exp(lg - jnp.max(lg, axis=0, keepdims=True))
    lb = e[0:1] / jnp.sum(e, axis=0, keepdims=True)
    f = lb + (1.0 - lb) * _sigmoid(f_in.astype(F32))
    kk = 1.0 - f
    bcum = jnp.where(live, jnp.log(f), 0.0)
    k = 1
    while k < TT:
        bcum = bcum + jnp.where(row >= k, pltpu.roll(bcum, k, 0), 0.0)
        k *= 2
    bend = bcum[TT - 1:TT]
    q = q_in.astype(F32)
    qs = q * _sigmoid(q)
    qhat = (qs * jnp.exp(bcum)).astype(BF16)
    khat = (kk * jnp.exp(bend - bcum)).astype(BF16)
    dec = jnp.exp(bend)
    qts, kts = [], []
    for r0, r1, mid in blocks:
        bref = bcum[mid:mid + 1]
        qts.append((qs[r0:r1] * jnp.exp(bcum[r0:r1] - bref)).astype(BF16))
        kt = kk * jnp.exp(bref - bcum)
        kts.append((kt if r1 == TT else jnp.where(row < r1, kt, 0.0)).astype(BF16))

    tri = lax.broadcasted_iota(jnp.int32, (TT, TT), 0) >= lax.broadcasted_iota(jnp.int32, (TT, TT), 1)
    o_inter = [lax.dot_general(qhat[:, sl], st_ref[h].astype(BF16), NT_DIMS, preferred_element_type=F32)
               for h, sl in enumerate(heads)]
    scores = [[lax.dot_general(qt[:, sl], kt[:, sl], NT_DIMS, preferred_element_type=F32)
               for qt, kt in zip(qts, kts)] for sl in heads]
    outs = []
    for h, sl in enumerate(heads):
        p = jnp.where(tri, jnp.concatenate(scores[h], axis=0), 0.0).astype(BF16)
        outs.append(jnp.dot(p, v_in[:, sl], preferred_element_type=F32) + o_inter[h])
    for h, sl in enumerate(heads):
        st_ref[h] = st_ref[h] * dec[:, sl] + lax.dot_general(v_in[:, sl], khat[:, sl], TN_DIMS,
                                                             preferred_element_type=F32)
    for h, sl in enumerate(heads):
        o = outs[h]
        o = o * lax.rsqrt(jnp.mean(o * o, axis=-1, keepdims=True) + EPS)
        og = g_in[:, sl].astype(F32)
        o_ref[:, sl] = (o * ng_ref[:, sl] * (og * _sigmoid(og))).astype(o_ref.dtype)


def _hgrn(proj, proj_meta, batch, seq, n_pad, col0, lb_logits, norm_g):
    n = batch * seq
    width = norm_g.shape[1]
    tiles = seq // TT
    in_row = lambda b, t: b * tiles + jnp.maximum(t - 1, 0)

    col = lambda c: pl.BlockSpec((TT, width), lambda b, t: (in_row(b, t), col0 + c))
    meta_col = lambda c: pl.BlockSpec((TT, width), lambda b, t: (0, col0 + c))
    return pl.pallas_call(
        functools.partial(_hgrn_kernel, n_pad),
        grid=(batch, tiles + 1),
        in_specs=[
            col(0), col(1), col(2), col(3),
            meta_col(0), meta_col(1), meta_col(2), meta_col(3),
            pl.BlockSpec(lb_logits.shape, lambda b, t: (0, 0)),
            pl.BlockSpec((1, width), lambda b, t: (0, 0)),
        ],
        out_specs=pl.BlockSpec((TT, width), lambda b, t: (in_row(b, t), 0)),
        out_shape=jax.ShapeDtypeStruct((n, width), BF16),
        scratch_shapes=[pltpu.VMEM((width // HEAD, HEAD, HEAD), F32)],
        compiler_params=_params(("parallel", "arbitrary")),
        name="hgrn2",
    )(proj, proj, proj, proj, proj_meta, proj_meta, proj_meta, proj_meta, lb_logits, norm_g)


def _merge_kernel(ya_ref, yb_ref, wpa_ref, wpb_ref, za_ref, zb_ref, o_ref):
    ta = jnp.dot(ya_ref[...], wpa_ref[...], preferred_element_type=F32)
    tb = jnp.dot(yb_ref[...], wpb_ref[...], preferred_element_type=F32)
    mixed = _sigmoid(za_ref[...].astype(F32)) * ta + _sigmoid(zb_ref[...].astype(F32)) * tb
    o_ref[...] = mixed.astype(o_ref.dtype)


def _merge(ya, yb, wpa, wpb, proj, zcol0, tm, tn):
    n, d = ya.shape
    dm = wpa.shape[1]
    nj = dm // tn
    return pl.pallas_call(
        _merge_kernel,
        grid=(n // tm, nj),
        in_specs=[
            pl.BlockSpec((tm, d), lambda i, j: (i, 0)),
            pl.BlockSpec((tm, d), lambda i, j: (i, 0)),
            pl.BlockSpec((d, tn), lambda i, j: (0, j)),
            pl.BlockSpec((d, tn), lambda i, j: (0, j)),
            pl.BlockSpec((tm, tn), lambda i, j: (i, zcol0 + j)),
            pl.BlockSpec((tm, tn), lambda i, j: (i, zcol0 + nj + j)),
        ],
        out_specs=pl.BlockSpec((tm, tn), lambda i, j: (i, j)),
        out_shape=jax.ShapeDtypeStruct((n, dm), BF16),
        compiler_params=_params(("parallel", "arbitrary")),
        name="merge",
    )(ya, yb, wpa, wpb, proj, proj)


def _outproj_kernel(m_ref, w_ref, x_ref, g_ref, h_ref, xnt_ref):
    h = x_ref[...] + jnp.dot(m_ref[...], w_ref[...], preferred_element_type=F32)
    h_ref[...] = h
    ms = jnp.mean(h * h, axis=-1, keepdims=True)
    xn = h * lax.rsqrt(ms + EPS) * g_ref[...]
    xnt_ref[...] = xn.T.astype(xnt_ref.dtype)


def _outproj(mixed, w_out, x, g, tm):
    n, d = x.shape
    return pl.pallas_call(
        _outproj_kernel,
        grid=(n // tm,),
        in_specs=[
            pl.BlockSpec((tm, d), lambda i: (i, 0)),
            pl.BlockSpec((d, d), lambda i: (0, 0)),
            pl.BlockSpec((tm, d), lambda i: (i, 0)),
            pl.BlockSpec((1, d), lambda i: (0, 0)),
        ],
        out_specs=[pl.BlockSpec((tm, d), lambda i: (i, 0)), pl.BlockSpec((d, tm), lambda i: (0, i))],
        out_shape=[jax.ShapeDtypeStruct((n, d), F32), jax.ShapeDtypeStruct((d, n), BF16)],
        compiler_params=_params(("parallel",)),
        name="outproj",
    )(mixed, w_out, x, g)


def _odd_even_merge_sort_pairs(n):
    pairs = []

    def merge(lo, cnt, r):
        step = r * 2
        if step < cnt:
            merge(lo, cnt, step)
            merge(lo + r, cnt, step)
            pairs.extend((i, i + r) for i in range(lo + r, lo + cnt - r, step))
        else:
            pairs.append((lo, lo + r))

    def sort(lo, cnt):
        if cnt > 1:
            sort(lo, cnt // 2)
            sort(lo + cnt // 2, cnt // 2)
            merge(lo, cnt, 1)

    sort(0, n)
    return pairs


def _order(v, i, j):
    v[i], v[j] = jnp.maximum(v[i], v[j]), jnp.minimum(v[i], v[j])


def _top_sorted(s):
    n = s.shape[0] // SUBLANES
    v = [s[SUBLANES * i:SUBLANES * (i + 1)] for i in range(n)]
    for i, j in _odd_even_merge_sort_pairs(n):
        _order(v, i, j)
    shift = SUBLANES // 2
    while shift:
        v = [jnp.maximum(v[i], pltpu.roll(v[n - 1 - i], shift, 0)) for i in range(n)]
        d = n // 2
        while d:
            for i in range(n):
                if not i & d:
                    _order(v, i, i + d)
            d //= 2
        shift //= 2
    return v


def _prefix_len(t, test):
    b3 = test(t[7])
    b2 = test(jnp.where(b3, t[11], t[3]))
    b1 = test(jnp.where(b3, jnp.where(b2, t[13], t[9]), jnp.where(b2, t[5], t[1])))
    hi = jnp.where(b2, jnp.where(b1, t[14], t[12]), jnp.where(b1, t[10], t[8]))
    lo = jnp.where(b2, jnp.where(b1, t[6], t[4]), jnp.where(b1, t[2], t[0]))
    b0 = test(jnp.where(b3, hi, lo))
    p = jnp.where(b3, 8.0, 0.0) + jnp.where(b2, 4.0, 0.0) + jnp.where(b1, 2.0, 0.0) + jnp.where(b0, 1.0, 0.0)
    return p + jnp.where(test(t[15]), 1.0, 0.0)


def _retrieve_kernel(xnt_ref, wqt_ref, keys_ref, r2_ref, e2_ref, m_ref, w_ref):
    heads = keys_ref.shape[0]
    dq = keys_ref.shape[3]
    k = PEER_TOPK
    tm = xnt_ref.shape[1]
    for h in range(heads):
        qt = jnp.dot(wqt_ref[2 * h * dq:(2 * h + 2) * dq, :], xnt_ref[...], preferred_element_type=F32)
        qt = qt.astype(BF16)
        s1_all = jnp.dot(keys_ref[h, 0], qt[:dq], preferred_element_type=F32)
        s2_all = jnp.dot(keys_ref[h, 1], qt[dq:], preferred_element_type=F32)
        for lo in range(0, tm, LANES):
            cols = slice(lo, lo + LANES)
            s1, s2 = s1_all[:, cols], s2_all[:, cols]
            top1, top2 = _top_sorted(s1), _top_sorted(s2)
            t1 = [t[0:1] for t in top1]
            t2 = [t[0:1] for t in top2]
            t2s = jnp.concatenate(t2, axis=0)
            pieces = [t1[i] + t2s[:k // (i + 1)] for i in range(k)]
            rows = sum(p.shape[0] for p in pieces)
            pieces.append(jnp.full((-rows % SUBLANES, LANES), -jnp.inf, F32))
            work = jnp.concatenate(pieces, axis=0)
            tau = None
            for _ in range(k):
                tau = jnp.max(work, axis=0, keepdims=True)
                work = jnp.where(work == tau, -jnp.inf, work)
            e2s = jnp.exp(t2s - t2[0])
            z = jnp.zeros_like(tau)
            for i in range(k):
                sel = (t1[i] + t2s) >= tau
                z = z + jnp.exp(t1[i] - t1[0]) * jnp.sum(jnp.where(sel, e2s, 0.0), axis=0, keepdims=True)
            rows8 = [slice(SUBLANES * i, SUBLANES * (i + 1)) for i in range(N_KEYS // SUBLANES)]
            r2 = jnp.concatenate([_prefix_len(top2, lambda t, x=s2[r]: t > x) for r in rows8], axis=0)
            m = jnp.concatenate([_prefix_len(top2, lambda t, x=s1[r]: (x + t) >= tau) for r in rows8], axis=0)
            r2_ref[h, :, cols] = r2.astype(r2_ref.dtype)
            e2_ref[h, :, cols] = jnp.exp(s2 - t2[0]).astype(e2_ref.dtype)
            m_ref[h, :, cols] = m
            w_ref[h, :, cols] = (0.5 * jnp.exp(s1 - t1[0])) / z


def _retrieve(xnt, wqt, keys, tm):
    d, n = xnt.shape
    heads = keys.shape[0]
    assert tm % LANES == 0
    assert PEER_TOPK == 16 and keys.shape[2] == N_KEYS == SUBLANES * PEER_TOPK
    big = lambda: pl.BlockSpec((heads, N_KEYS, tm), lambda i: (0, 0, i))
    shape = lambda dt: jax.ShapeDtypeStruct((heads, N_KEYS, n), dt)
    return pl.pallas_call(
        _retrieve_kernel,
        grid=(n // tm,),
        in_specs=[
            pl.BlockSpec((d, tm), lambda i: (0, i)),
            pl.BlockSpec(wqt.shape, lambda i: (0, 0)),
            pl.BlockSpec(keys.shape, lambda i: (0, 0, 0, 0)),
        ],
        out_specs=[big(), big(), big(), big()],
        out_shape=[shape(BF16), shape(BF16), shape(F32), shape(F32)],
        compiler_params=_params(("parallel",)),
        name="peer_retrieve",
    )(xnt, wqt, keys)


def _experts_kernel(n_blk, xnt_ref, u_ref, vt_ref, r2_ref, e2_ref, m_ref, w_ref, h_ref, g_ref, o_ref,
                    act0_ref, act1_ref, acc_ref):
    s = pl.program_id(0)
    heads = r2_ref.shape[0]
    ec = u_ref.shape[0]
    chunks = ec // N_KEYS
    e_prev = jnp.maximum(s - 1, 0) % n_blk

    @pl.when(s == 0)
    def _():
        act1_ref[...] = jnp.zeros_like(act1_ref)

    @pl.when(e_prev == 0)
    def _():
        acc_ref[...] = jnp.zeros_like(acc_ref)

    def step(src_ref, dst_ref):
        c0 = e_prev * chunks
        mrows = [[m_ref[h, pl.ds(c0 + cc, 1), :].astype(BF16) for h in range(heads)] for cc in range(chunks)]
        wrows = [[w_ref[h, pl.ds(c0 + cc, 1), :].astype(BF16) for h in range(heads)] for cc in range(chunks)]
        dst_ref[...] = jnp.dot(u_ref[...].astype(BF16), xnt_ref[...], preferred_element_type=F32)
        ws = []
        for cc in range(chunks):
            gate = None
            for h in range(heads):
                gh = jnp.where(r2_ref[h] < mrows[cc][h], e2_ref[h], 0) * wrows[cc][h]
                gate = gh if gate is None else gate + gh
            a = src_ref[cc * N_KEYS:(cc + 1) * N_KEYS, :]
            ge = a * (1.0 + lax.erf(a * (1.0 / math.sqrt(2.0))))
            ws.append(ge.astype(BF16) * gate)
        wg = jnp.concatenate(ws, axis=0)
        acc_ref[...] += jnp.dot(vt_ref[...], wg, preferred_element_type=F32)

    @pl.when(s % 2 == 0)
    def _():
        step(act1_ref, act0_ref)

    @pl.when(s % 2 == 1)
    def _():
        step(act0_ref, act1_ref)

    @pl.when(jnp.logical_and(s > 0, e_prev == n_blk - 1))
    def _():
        hh = h_ref[...] + acc_ref[...].T
        ms = jnp.mean(hh * hh, axis=-1, keepdims=True)
        o_ref[...] = hh * lax.rsqrt(ms + EPS) * g_ref[...]


def _experts(xnt, u, vt, r2, e2, m, w, h2, g, tm, ec):
    d, n = xnt.shape
    n_exp = u.shape[0]
    heads = r2.shape[0]
    n_blk = n_exp // ec
    assert ec % N_KEYS == 0
    steps = (n // tm) * n_blk
    cur = lambda s: jnp.minimum(s, steps - 1)
    prev = lambda s: jnp.maximum(s - 1, 0)
    tile_spec = lambda shape, imap: pl.BlockSpec(shape, imap, pipeline_mode=pl.Buffered(1))
    big = lambda: tile_spec((heads, N_KEYS, tm), lambda s: (0, 0, prev(s) // n_blk))
    return pl.pallas_call(
        functools.partial(_experts_kernel, n_blk),
        grid=(steps + 1,),
        in_specs=[
            pl.BlockSpec((d, tm), lambda s: (0, cur(s) // n_blk)),
            pl.BlockSpec((ec, d), lambda s: (cur(s) % n_blk, 0)),
            pl.BlockSpec((d, ec), lambda s: (0, prev(s) % n_blk)),
            big(), big(), big(), big(),
            tile_spec((tm, d), lambda s: (prev(s) // n_blk, 0)),
            pl.BlockSpec((1, d), lambda s: (0, 0)),
        ],
        out_specs=tile_spec((tm, d), lambda s: (prev(s) // n_blk, 0)),
        out_shape=jax.ShapeDtypeStruct((n, d), F32),
        scratch_shapes=[pltpu.VMEM((ec, tm), F32), pltpu.VMEM((ec, tm), F32), pltpu.VMEM((d, tm), F32)],
        compiler_params=_params(("arbitrary",)),
        name="peer_experts",
    )(xnt, u, vt, r2, e2, m, w, h2, g)


def _transpose_cast_kernel(x_ref, o_ref):
    o_ref[...] = x_ref[...].T.astype(o_ref.dtype)


def _transpose_cast(x, tr):
    rows, cols = x.shape
    return pl.pallas_call(
        _transpose_cast_kernel,
        grid=(rows // tr,),
        in_specs=[pl.BlockSpec((tr, cols), lambda i: (i, 0))],
        out_specs=pl.BlockSpec((cols, tr), lambda i: (0, i)),
        out_shape=jax.ShapeDtypeStruct((cols, rows), BF16),
        compiler_params=_params(("parallel",)),
        name="transpose_cast",
    )(x)


def _tile(n, want):
    t = min(n, want)
    assert n % t == 0, (n, t)
    return t


def kernel(x, meta, ln1_g, w_in, conv_w, conv_b, rg_wa, rg_ba, rg_wx, rg_bx, rg_lambda, hg_lb_logits,
           hg_norm_g, w_pa, w_pb, w_out, ln2_g, peer_wq, peer_keys, peer_u, peer_v, final_g):
    batch, seq, d = x.shape
    n = batch * seq
    n_meta = meta.shape[0]
    depth = w_in.shape[0]
    assert depth == 1 and seq % TT == 0 and n_meta <= TT
    rg_width = conv_w.shape[2]
    hg_width = hg_norm_g.shape[1]
    assert rg_width == hg_width == d and rg_wa.shape[2] == HEAD and d % PROJ_COLS == 0
    n_pad = TT - n_meta
    row = lambda a: a.reshape(1, -1)

    xf = x.reshape(n, d)
    meta_tile = jnp.concatenate([jnp.zeros((n_pad, d), x.dtype), meta.astype(x.dtype)], axis=0)

    w_in_b = w_in[0].astype(BF16)
    tm = _tile(n, PROJ_ROWS)
    proj = _norm_matmul(xf, row(ln1_g[0]), w_in_b, tm, PROJ_COLS)
    proj_meta = _norm_matmul(meta_tile, row(ln1_g[0]), w_in_b, TT, PROJ_COLS)

    y_a = _rglru(proj, proj_meta, batch, seq, n_pad, conv_w[0], row(conv_b[0]), rg_wa[0].astype(BF16),
                 row(rg_ba[0]), rg_wx[0].astype(BF16), row(rg_bx[0]), row(rg_lambda[0]))
    y_b = _hgrn(proj, proj_meta, batch, seq, n_pad, 2, hg_lb_logits, row(hg_norm_g[0]))

    mixed = _merge(y_a, y_b, w_pa[0].astype(BF16), w_pb[0].astype(BF16), proj, 6 * (d // PROJ_COLS), tm,
                   PROJ_COLS)
    h2, xn2t = _outproj(mixed, w_out[0].astype(BF16), xf, row(ln2_g[0]), _tile(n, OUT_ROWS))

    tp = _tile(n, PEER_TOKENS)
    r2, e2, m, w = _retrieve(xn2t, peer_wq[0].T.astype(BF16), peer_keys[0].astype(BF16), tp)
    out = _experts(xn2t, peer_u[0], _transpose_cast(peer_v[0], CAST_ROWS), r2, e2, m, w, h2,
                   row(final_g), tp, PEER_EXPERTS)
    return out.reshape(batch, seq, d)
```

```python
import functools
import math

import jax
import jax.numpy as jnp
from jax import lax
from jax.experimental import pallas as pl
from jax.experimental.pallas import tpu as pltpu

F32 = jnp.float32
BF16 = jnp.bfloat16

EPS = 1e-6
RG_C = 8.0
CONV_WIDTH = 4
HEAD = 128
N_KEYS = 128
PEER_TOPK = 16
LANES = 128
SUBLANES = 8
VMEM_LIMIT = 56 * 1024 * 1024
EXPERTS_VMEM_LIMIT = 60 * 1024 * 1024
WEIGHT_SLOTS = 3

TT = 128
RB = 32
PROJ_ROWS = 1024
PROJ_COLS = 1024
OUT_ROWS = 512
PEER_TOKENS = 512
PEER_EXPERTS = 1024
CAST_ROWS = 512

NT_DIMS = (((1,), (1,)), ((), ()))
TN_DIMS = (((0,), (0,)), ((), ()))


def _params(sem):
    return pltpu.CompilerParams(dimension_semantics=sem, vmem_limit_bytes=VMEM_LIMIT)


def _sigmoid(x):
    return jax.nn.sigmoid(x)


def _gelu(x):
    return 0.5 * x * (1.0 + lax.erf(x * (1.0 / math.sqrt(2.0))))


def _norm_mm_kernel(x_ref, g_ref, w_ref, o_ref, xn_ref):
    @pl.when(pl.program_id(1) == 0)
    def _():
        xf = x_ref[...]
        ms = jnp.mean(xf * xf, axis=-1, keepdims=True)
        xn_ref[...] = (xf * lax.rsqrt(ms + EPS) * g_ref[...]).astype(BF16)

    o_ref[...] = jnp.dot(xn_ref[...], w_ref[...], preferred_element_type=F32).astype(o_ref.dtype)


def _norm_matmul(x, g, w, tm, tn):
    m, d = x.shape
    n = w.shape[1]
    return pl.pallas_call(
        _norm_mm_kernel,
        grid=(m // tm, n // tn),
        in_specs=[
            pl.BlockSpec((tm, d), lambda i, j: (i, 0)),
            pl.BlockSpec((1, d), lambda i, j: (0, 0)),
            pl.BlockSpec((d, tn), lambda i, j: (0, j)),
        ],
        out_specs=pl.BlockSpec((tm, tn), lambda i, j: (i, j)),
        out_shape=jax.ShapeDtypeStruct((m, n), BF16),
        scratch_shapes=[pltpu.VMEM((tm, d), BF16)],
        compiler_params=_params(("parallel", "arbitrary")),
        name="norm_matmul",
    )(x, g, w)


def _rglru_kernel(n_pad, xa_ref, ya_ref, xam_ref, yam_ref, cw_ref, cb_ref, wa_ref, ba_ref, wx_ref, bx_ref,
                  lam_ref, o_ref, xbuf_ref, hc_ref):
    t = pl.program_id(1)
    width = xa_ref.shape[1]
    is_meta = t == 0
    halo = SUBLANES

    @pl.when(is_meta)
    def _():
        xbuf_ref[0:halo, :] = jnp.zeros((halo, width), F32)
        hc_ref[...] = jnp.zeros_like(hc_ref)

    xbuf_ref[halo:halo + TT, :] = jnp.where(is_meta, xam_ref[...], xa_ref[...]).astype(F32)
    row = lax.broadcasted_iota(jnp.int32, (TT, HEAD), 0)
    live = jnp.logical_or(t > 0, row >= n_pad)
    r8 = row & (SUBLANES - 1)

    for h in range(width // HEAD):
        sl = slice(h * HEAD, (h + 1) * HEAD)
        x0 = halo - (CONV_WIDTH - 1)
        xc = cb_ref[:, sl] + cw_ref[0:1, sl] * xbuf_ref[x0:x0 + TT, sl]
        for k in range(1, CONV_WIDTH):
            xc = xc + cw_ref[k:k + 1, sl] * xbuf_ref[x0 + k:x0 + k + TT, sl]
        xcb = xc.astype(BF16)
        r = _sigmoid(jnp.dot(xcb, wa_ref[h], preferred_element_type=F32) + ba_ref[:, sl])
        i = _sigmoid(jnp.dot(xcb, wx_ref[h], preferred_element_type=F32) + bx_ref[:, sl])
        sp = jax.nn.softplus(-lam_ref[:, sl])
        log_a = (-RG_C) * r * sp
        a = jnp.exp(log_a)
        th = jnp.tanh(log_a)
        u = jnp.sqrt(-2.0 * th / (1.0 - th)) * (i * xc)
        u = jnp.where(live, u, 0.0)
        for k in (1, 2, 4):
            a_sh = pltpu.roll(a, k, 0)
            u_sh = pltpu.roll(u, k, 0)
            m = r8 >= k
            u = jnp.where(m, a * u_sh + u, u)
            a = jnp.where(m, a * a_sh, a)
        carry = hc_ref[:, sl]
        outs = []
        for g in range(TT // SUBLANES):
            rows = slice(g * SUBLANES, (g + 1) * SUBLANES)
            hg = a[rows] * carry + u[rows]
            outs.append(hg)
            carry = hg[SUBLANES - 1:SUBLANES]
        hc_ref[:, sl] = carry
        hs = jnp.concatenate(outs, axis=0)
        ya = jnp.where(is_meta, yam_ref[:, sl], ya_ref[:, sl]).astype(F32)
        o_ref[:, sl] = (_gelu(ya) * hs).astype(o_ref.dtype)

    xbuf_ref[0:halo, :] = xbuf_ref[TT:TT + halo, :]


def _rglru(proj, proj_meta, batch, seq, n_pad, conv_w, conv_b, wa, ba, wx, bx, lam):
    n = batch * seq
    width = conv_w.shape[1]
    tiles = seq // TT
    in_row = lambda b, t: b * tiles + jnp.maximum(t - 1, 0)

    vec = lambda: pl.BlockSpec((1, width), lambda b, t: (0, 0))
    gate_w = lambda: pl.BlockSpec((width // HEAD, HEAD, HEAD), lambda b, t: (0, 0, 0))
    return pl.pallas_call(
        functools.partial(_rglru_kernel, n_pad),
        grid=(batch, tiles + 1),
        in_specs=[
            pl.BlockSpec((TT, width), lambda b, t: (in_row(b, t), 0)),
            pl.BlockSpec((TT, width), lambda b, t: (in_row(b, t), 1)),
            pl.BlockSpec((TT, width), lambda b, t: (0, 0)),
            pl.BlockSpec((TT, width), lambda b, t: (0, 1)),
            pl.BlockSpec((CONV_WIDTH, width), lambda b, t: (0, 0)),
            vec(), gate_w(), vec(), gate_w(), vec(), vec(),
        ],
        out_specs=pl.BlockSpec((TT, width), lambda b, t: (in_row(b, t), 0)),
        out_shape=jax.ShapeDtypeStruct((n, width), BF16),
        scratch_shapes=[pltpu.VMEM((TT + SUBLANES, width), F32), pltpu.VMEM((1, width), F32)],
        compiler_params=_params(("parallel", "arbitrary")),
        name="rglru",
    )(proj, proj, proj_meta, proj_meta, conv_w, conv_b, wa, ba, wx, bx, lam)


def _hgrn_kernel(n_pad, q_ref, f_ref, v_ref, g_ref, qm_ref, fm_ref, vm_ref, gm_ref, lbl_ref, ng_ref, o_ref,
                 st_ref):
    t = pl.program_id(1)
    width = q_ref.shape[1]
    is_meta = t == 0

    @pl.when(is_meta)
    def _():
        st_ref[...] = jnp.zeros_like(st_ref)

    pick = lambda meta_ref, ref: jnp.where(is_meta, meta_ref[...], ref[...])
    q_in, f_in, v_in, g_in = pick(qm_ref, q_ref), pick(fm_ref, f_ref), pick(vm_ref, v_ref), pick(gm_ref, g_ref)

    heads = [slice(h * HEAD, (h + 1) * HEAD) for h in range(width // HEAD)]
    blocks = [(r0, r0 + RB, r0 + RB // 2 - 1) for r0 in range(0, TT, RB)]

    row = lax.broadcasted_iota(jnp.int32, (TT, width), 0)
    live = jnp.logical_or(t > 0, row >= n_pad)
    lg = lbl_ref[...]
    e = jnp.exp(lg - jnp.max(lg, axis=0, keepdims=True))
    lb = e[0:1] / jnp.sum(e, axis=0, keepdims=True)
    f = lb + (1.0 - lb) * _sigmoid(f_in.astype(F32))
    kk = 1.0 - f
    bcum = jnp.where(live, jnp.log(f), 0.0)
    k = 1
    while k < TT:
        bcum = bcum + jnp.where(row >= k, pltpu.roll(bcum, k, 0), 0.0)
        k *= 2
    bend = bcum[TT - 1:TT]
    q = q_in.astype(F32)
    qs = q * _sigmoid(q)
    qhat = (qs * jnp.exp(bcum)).astype(BF16)
    khat = (kk * jnp.exp(bend - bcum)).astype(BF16)
    dec = jnp.exp(bend)
    qts, kts = [], []
    for r0, r1, mid in blocks:
        bref = bcum[mid:mid + 1]
        qts.append((qs[r0:r1] * jnp.exp(bcum[r0:r1] - bref)).astype(BF16))
        kt = kk * jnp.exp(bref - bcum)
        kts.append((kt if r1 == TT else jnp.where(row < r1, kt, 0.0)).astype(BF16))

    tri = lax.broadcasted_iota(jnp.int32, (TT, TT), 0) >= lax.broadcasted_iota(jnp.int32, (TT, TT), 1)
    o_inter = [lax.dot_general(qhat[:, sl], st_ref[h].astype(BF16), NT_DIMS, preferred_element_type=F32)
               for h, sl in enumerate(heads)]
    scores = [[lax.dot_general(qt[:, sl], kt[:, sl], NT_DIMS, preferred_element_type=F32)
               for qt, kt in zip(qts, kts)] for sl in heads]
    outs = []
    for h, sl in enumerate(heads):
        p = jnp.where(tri, jnp.concatenate(scores[h], axis=0), 0.0).astype(BF16)
        outs.append(jnp.dot(p, v_in[:, sl], preferred_element_type=F32) + o_inter[h])
    for h, sl in enumerate(heads):
        st_ref[h] = st_ref[h] * dec[:, sl] + lax.dot_general(v_in[:, sl], khat[:, sl], TN_DIMS,
                                                             preferred_element_type=F32)
    for h, sl in enumerate(heads):
        o = outs[h]
        o = o * lax.rsqrt(jnp.mean(o * o, axis=-1, keepdims=True) + EPS)
        og = g_in[:, sl].astype(F32)
        o_ref[:, sl] = (o * ng_ref[:, sl] * (og * _sigmoid(og))).astype(o_ref.dtype)


def _hgrn(proj, proj_meta, batch, seq, n_pad, col0, lb_logits, norm_g):
    n = batch * seq
    width = norm_g.shape[1]
    tiles = seq // TT
    in_row = lambda b, t: b * tiles + jnp.maximum(t - 1, 0)

    col = lambda c: pl.BlockSpec((TT, width), lambda b, t: (in_row(b, t), col0 + c))
    meta_col = lambda c: pl.BlockSpec((TT, width), lambda b, t: (0, col0 + c))
    return pl.pallas_call(
        functools.partial(_hgrn_kernel, n_pad),
        grid=(batch, tiles + 1),
        in_specs=[
            col(0), col(1), col(2), col(3),
            meta_col(0), meta_col(1), meta_col(2), meta_col(3),
            pl.BlockSpec(lb_logits.shape, lambda b, t: (0, 0)),
            pl.BlockSpec((1, width), lambda b, t: (0, 0)),
        ],
        out_specs=pl.BlockSpec((TT, width), lambda b, t: (in_row(b, t), 0)),
        out_shape=jax.ShapeDtypeStruct((n, width), BF16),
        scratch_shapes=[pltpu.VMEM((width // HEAD, HEAD, HEAD), F32)],
        compiler_params=_params(("parallel", "arbitrary")),
        name="hgrn2",
    )(proj, proj, proj, proj, proj_meta, proj_meta, proj_meta, proj_meta, lb_logits, norm_g)


def _merge_kernel(ya_ref, yb_ref, wpa_ref, wpb_ref, za_ref, zb_ref, o_ref):
    ta = jnp.dot(ya_ref[...], wpa_ref[...], preferred_element_type=F32)
    tb = jnp.dot(yb_ref[...], wpb_ref[...], preferred_element_type=F32)
    mixed = _sigmoid(za_ref[...].astype(F32)) * ta + _sigmoid(zb_ref[...].astype(F32)) * tb
    o_ref[...] = mixed.astype(o_ref.dtype)


def _merge(ya, yb, wpa, wpb, proj, zcol0, tm, tn):
    n, d = ya.shape
    dm = wpa.shape[1]
    nj = dm // tn
    return pl.pallas_call(
        _merge_kernel,
        grid=(n // tm, nj),
        in_specs=[
            pl.BlockSpec((tm, d), lambda i, j: (i, 0)),
            pl.BlockSpec((tm, d), lambda i, j: (i, 0)),
            pl.BlockSpec((d, tn), lambda i, j: (0, j)),
            pl.BlockSpec((d, tn), lambda i, j: (0, j)),
            pl.BlockSpec((tm, tn), lambda i, j: (i, zcol0 + j)),
            pl.BlockSpec((tm, tn), lambda i, j: (i, zcol0 + nj + j)),
        ],
        out_specs=pl.BlockSpec((tm, tn), lambda i, j: (i, j)),
        out_shape=jax.ShapeDtypeStruct((n, dm), BF16),
        compiler_params=_params(("parallel", "arbitrary")),
        name="merge",
    )(ya, yb, wpa, wpb, proj, proj)


def _outproj_kernel(m_ref, w_ref, x_ref, g_ref, h_ref, xnt_ref):
    h = x_ref[...] + jnp.dot(m_ref[...], w_ref[...], preferred_element_type=F32)
    h_ref[...] = h
    ms = jnp.mean(h * h, axis=-1, keepdims=True)
    xn = h * lax.rsqrt(ms + EPS) * g_ref[...]
    xnt_ref[...] = xn.T.astype(xnt_ref.dtype)


def _outproj(mixed, w_out, x, g, tm):
    n, d = x.shape
    return pl.pallas_call(
        _outproj_kernel,
        grid=(n // tm,),
        in_specs=[
            pl.BlockSpec((tm, d), lambda i: (i, 0)),
            pl.BlockSpec((d, d), lambda i: (0, 0)),
            pl.BlockSpec((tm, d), lambda i: (i, 0)),
            pl.BlockSpec((1, d), lambda i: (0, 0)),
        ],
        out_specs=[pl.BlockSpec((tm, d), lambda i: (i, 0)), pl.BlockSpec((d, tm), lambda i: (0, i))],
        out_shape=[jax.ShapeDtypeStruct((n, d), F32), jax.ShapeDtypeStruct((d, n), BF16)],
        compiler_params=_params(("parallel",)),
        name="outproj",
    )(mixed, w_out, x, g)


def _odd_even_merge_sort_pairs(n):
    pairs = []

    def merge(lo, cnt, r):
        step = r * 2
        if step < cnt:
            merge(lo, cnt, step)
            merge(lo + r, cnt, step)
            pairs.extend((i, i + r) for i in range(lo + r, lo + cnt - r, step))
        else:
            pairs.append((lo, lo + r))

    def sort(lo, cnt):
        if cnt > 1:
            sort(lo, cnt // 2)
            sort(lo + cnt // 2, cnt // 2)
            merge(lo, cnt, 1)

    sort(0, n)
    return pairs


def _order(v, i, j):
    v[i], v[j] = jnp.maximum(v[i], v[j]), jnp.minimum(v[i], v[j])


def _top_sorted(s):
    n = s.shape[0] // SUBLANES
    v = [s[SUBLANES * i:SUBLANES * (i + 1)] for i in range(n)]
    for i, j in _odd_even_merge_sort_pairs(n):
        _order(v, i, j)
    shift = SUBLANES // 2
    while shift:
        v = [jnp.maximum(v[i], pltpu.roll(v[n - 1 - i], shift, 0)) for i in range(n)]
        d = n // 2
        while d:
            for i in range(n):
                if not i & d:
                    _order(v, i, i + d)
            d //= 2
        shift //= 2
    return v


def _prefix_len(t, test):
    b3 = test(t[7])
    b2 = test(jnp.where(b3, t[11], t[3]))
    b1 = test(jnp.where(b3, jnp.where(b2, t[13], t[9]), jnp.where(b2, t[5], t[1])))
    hi = jnp.where(b2, jnp.where(b1, t[14], t[12]), jnp.where(b1, t[10], t[8]))
    lo = jnp.where(b2, jnp.where(b1, t[6], t[4]), jnp.where(b1, t[2], t[0]))
    b0 = test(jnp.where(b3, hi, lo))
    p = jnp.where(b3, 8.0, 0.0) + jnp.where(b2, 4.0, 0.0) + jnp.where(b1, 2.0, 0.0) + jnp.where(b0, 1.0, 0.0)
    return p + jnp.where(test(t[15]), 1.0, 0.0)


def _retrieve_kernel(xnt_ref, wqt_ref, keys_ref, r2_ref, e2_ref, m_ref, w_ref):
    heads = keys_ref.shape[0]
    dq = keys_ref.shape[3]
    k = PEER_TOPK
    tm = xnt_ref.shape[1]
    for h in range(heads):
        qt = jnp.dot(wqt_ref[2 * h * dq:(2 * h + 2) * dq, :], xnt_ref[...], preferred_element_type=F32)
        qt = qt.astype(BF16)
        s1_all = jnp.dot(keys_ref[h, 0], qt[:dq], preferred_element_type=F32)
        s2_all = jnp.dot(keys_ref[h, 1], qt[dq:], preferred_element_type=F32)
        for lo in range(0, tm, LANES):
            cols = slice(lo, lo + LANES)
            s1, s2 = s1_all[:, cols], s2_all[:, cols]
            top1, top2 = _top_sorted(s1), _top_sorted(s2)
            t1 = [t[0:1] for t in top1]
            t2 = [t[0:1] for t in top2]
            t2s = jnp.concatenate(t2, axis=0)
            pieces = [t1[i] + t2s[:k // (i + 1)] for i in range(k)]
            rows = sum(p.shape[0] for p in pieces)
            pieces.append(jnp.full((-rows % SUBLANES, LANES), -jnp.inf, F32))
            work = jnp.concatenate(pieces, axis=0)
            tau = None
            for _ in range(k):
                tau = jnp.max(work, axis=0, keepdims=True)
                work = jnp.where(work == tau, -jnp.inf, work)
            e2s = jnp.exp(t2s - t2[0])
            z = jnp.zeros_like(tau)
            for i in range(k):
                sel = (t1[i] + t2s) >= tau
                z = z + jnp.exp(t1[i] - t1[0]) * jnp.sum(jnp.where(sel, e2s, 0.0), axis=0, keepdims=True)
            rows8 = [slice(SUBLANES * i, SUBLANES * (i + 1)) for i in range(N_KEYS // SUBLANES)]
            r2 = jnp.concatenate([_prefix_len(top2, lambda t, x=s2[r]: t > x) for r in rows8], axis=0)
            m = jnp.concatenate([_prefix_len(top2, lambda t, x=s1[r]: (x + t) >= tau) for r in rows8], axis=0)
            r2_ref[h, :, cols] = r2.astype(r2_ref.dtype)
            e2_ref[h, :, cols] = jnp.exp(s2 - t2[0]).astype(e2_ref.dtype)
            m_ref[h, :, cols] = m
            w_ref[h, :, cols] = (0.5 * jnp.exp(s1 - t1[0])) / z


def _retrieve(xnt, wqt, keys, tm):
    d, n = xnt.shape
    heads = keys.shape[0]
    assert tm % LANES == 0
    assert PEER_TOPK == 16 and keys.shape[2] == N_KEYS == SUBLANES * PEER_TOPK
    big = lambda: pl.BlockSpec((heads, N_KEYS, tm), lambda i: (0, 0, i))
    shape = lambda dt: jax.ShapeDtypeStruct((heads, N_KEYS, n), dt)
    return pl.pallas_call(
        _retrieve_kernel,
        grid=(n // tm,),
        in_specs=[
            pl.BlockSpec((d, tm), lambda i: (0, i)),
            pl.BlockSpec(wqt.shape, lambda i: (0, 0)),
            pl.BlockSpec(keys.shape, lambda i: (0, 0, 0, 0)),
        ],
        out_specs=[big(), big(), big(), big()],
        out_shape=[shape(BF16), shape(BF16), shape(F32), shape(F32)],
        compiler_params=_params(("parallel",)),
        name="peer_retrieve",
    )(xnt, wqt, keys)


def _experts_kernel(n_blk, steps, xnt_ref, u_hbm, vt_hbm, r2_ref, e2_ref, m_ref, w_ref, h_ref, g_ref, o_ref,
                    ubuf_ref, vbuf_ref, usem, vsem, act0_ref, act1_ref, acc_ref):
    s = pl.program_id(0)
    heads = r2_ref.shape[0]
    ec = ubuf_ref.shape[1]
    chunks = ec // N_KEYS
    e_prev = jnp.maximum(s - 1, 0) % n_blk

    def u_copy(j):
        slot = j % WEIGHT_SLOTS
        return pltpu.make_async_copy(u_hbm.at[pl.ds((j % n_blk) * ec, ec), :], ubuf_ref.at[slot], usem.at[slot])

    def v_copy(j):
        slot = j % WEIGHT_SLOTS
        return pltpu.make_async_copy(vt_hbm.at[:, pl.ds((j % n_blk) * ec, ec)], vbuf_ref.at[slot], vsem.at[slot])

    @pl.when(s == 0)
    def _():
        act1_ref[...] = jnp.zeros_like(act1_ref)
        u_copy(0).start()
        u_copy(1).start()
        v_copy(0).start()
        v_copy(0).wait()

    @pl.when(s + 2 < steps)
    def _():
        u_copy(s + 2).start()

    @pl.when(s + 1 < steps)
    def _():
        v_copy(s + 1).start()

    @pl.when(s < steps)
    def _():
        u_copy(s).wait()

    @pl.when(s > 1)
    def _():
        v_copy(s - 1).wait()

    u_ref = ubuf_ref.at[jnp.minimum(s, steps - 1) % WEIGHT_SLOTS]
    vt_ref = vbuf_ref.at[jnp.maximum(s - 1, 0) % WEIGHT_SLOTS]

    @pl.when(e_prev == 0)
    def _():
        acc_ref[...] = jnp.zeros_like(acc_ref)

    def step(src_ref, dst_ref):
        c0 = e_prev * chunks
        mrows = [[m_ref[h, pl.ds(c0 + cc, 1), :].astype(BF16) for h in range(heads)] for cc in range(chunks)]
        wrows = [[w_ref[h, pl.ds(c0 + cc, 1), :].astype(BF16) for h in range(heads)] for cc in range(chunks)]
        dst_ref[...] = jnp.dot(u_ref[...], xnt_ref[...], preferred_element_type=F32)
        ws = []
        for cc in range(chunks):
            gate = None
            for h in range(heads):
                gh = jnp.where(r2_ref[h] < mrows[cc][h], e2_ref[h], 0) * wrows[cc][h]
                gate = gh if gate is None else gate + gh
            a = src_ref[cc * N_KEYS:(cc + 1) * N_KEYS, :]
            ge = a * (1.0 + lax.erf(a * (1.0 / math.sqrt(2.0))))
            ws.append(ge.astype(BF16) * gate)
        wg = jnp.concatenate(ws, axis=0)
        acc_ref[...] += jnp.dot(vt_ref[...], wg, preferred_element_type=F32)

    @pl.when(s % 2 == 0)
    def _():
        step(act1_ref, act0_ref)

    @pl.when(s % 2 == 1)
    def _():
        step(act0_ref, act1_ref)

    @pl.when(jnp.logical_and(s > 0, e_prev == n_blk - 1))
    def _():
        hh = h_ref[...] + acc_ref[...].T
        ms = jnp.mean(hh * hh, axis=-1, keepdims=True)
        o_ref[...] = hh * lax.rsqrt(ms + EPS) * g_ref[...]


def _experts(xnt, u, vt, r2, e2, m, w, h2, g, tm, ec):
    d, n = xnt.shape
    n_exp = u.shape[0]
    heads = r2.shape[0]
    n_blk = n_exp // ec
    assert ec % N_KEYS == 0
    steps = (n // tm) * n_blk
    assert steps >= WEIGHT_SLOTS
    cur = lambda s: jnp.minimum(s, steps - 1)
    prev = lambda s: jnp.maximum(s - 1, 0)
    tile_spec = lambda shape, imap: pl.BlockSpec(shape, imap, pipeline_mode=pl.Buffered(1))
    big = lambda: tile_spec((heads, N_KEYS, tm), lambda s: (0, 0, prev(s) // n_blk))
    return pl.pallas_call(
        functools.partial(_experts_kernel, n_blk, steps),
        grid=(steps + 1,),
        in_specs=[
            pl.BlockSpec((d, tm), lambda s: (0, cur(s) // n_blk)),
            pl.BlockSpec(memory_space=pl.ANY),
            pl.BlockSpec(memory_space=pl.ANY),
            big(), big(), big(), big(),
            tile_spec((tm, d), lambda s: (prev(s) // n_blk, 0)),
            pl.BlockSpec((1, d), lambda s: (0, 0)),
        ],
        out_specs=pl.BlockSpec((tm, d), lambda s: (prev(s) // n_blk, 0)),
        out_shape=jax.ShapeDtypeStruct((n, d), F32),
        scratch_shapes=[pltpu.VMEM((WEIGHT_SLOTS, ec, d), BF16), pltpu.VMEM((WEIGHT_SLOTS, d, ec), BF16),
                        pltpu.SemaphoreType.DMA((WEIGHT_SLOTS,)), pltpu.SemaphoreType.DMA((WEIGHT_SLOTS,)),
                        pltpu.VMEM((ec, tm), F32), pltpu.VMEM((ec, tm), F32), pltpu.VMEM((d, tm), F32)],
        compiler_params=pltpu.CompilerParams(dimension_semantics=("arbitrary",),
                                             vmem_limit_bytes=EXPERTS_VMEM_LIMIT),
        name="peer_experts",
    )(xnt, u, vt, r2, e2, m, w, h2, g)


def _transpose_cast_kernel(x_ref, o_ref):
    o_ref[...] = x_ref[...].T.astype(o_ref.dtype)


def _transpose_cast(x, tr):
    rows, cols = x.shape
    return pl.pallas_call(
        _transpose_cast_kernel,
        grid=(rows // tr,),
        in_specs=[pl.BlockSpec((tr, cols), lambda i: (i, 0))],
        out_specs=pl.BlockSpec((cols, tr), lambda i: (0, i)),
        out_shape=jax.ShapeDtypeStruct((cols, rows), BF16),
        compiler_params=_params(("parallel",)),
        name="transpose_cast",
    )(x)


def _tile(n, want):
    t = min(n, want)
    assert n % t == 0, (n, t)
    return t


def kernel(x, meta, ln1_g, w_in, conv_w, conv_b, rg_wa, rg_ba, rg_wx, rg_bx, rg_lambda, hg_lb_logits,
           hg_norm_g, w_pa, w_pb, w_out, ln2_g, peer_wq, peer_keys, peer_u, peer_v, final_g):
    batch, seq, d = x.shape
    n = batch * seq
    n_meta = meta.shape[0]
    depth = w_in.shape[0]
    assert depth == 1 and seq % TT == 0 and n_meta <= TT
    rg_width = conv_w.shape[2]
    hg_width = hg_norm_g.shape[1]
    assert rg_width == hg_width == d and rg_wa.shape[2] == HEAD and d % PROJ_COLS == 0
    n_pad = TT - n_meta
    row = lambda a: a.reshape(1, -1)

    xf = x.reshape(n, d)
    meta_tile = jnp.concatenate([jnp.zeros((n_pad, d), x.dtype), meta.astype(x.dtype)], axis=0)

    w_in_b = w_in[0].astype(BF16)
    tm = _tile(n, PROJ_ROWS)
    proj = _norm_matmul(xf, row(ln1_g[0]), w_in_b, tm, PROJ_COLS)
    proj_meta = _norm_matmul(meta_tile, row(ln1_g[0]), w_in_b, TT, PROJ_COLS)

    y_a = _rglru(proj, proj_meta, batch, seq, n_pad, conv_w[0], row(conv_b[0]), rg_wa[0].astype(BF16),
                 row(rg_ba[0]), rg_wx[0].astype(BF16), row(rg_bx[0]), row(rg_lambda[0]))
    y_b = _hgrn(proj, proj_meta, batch, seq, n_pad, 2, hg_lb_logits, row(hg_norm_g[0]))

    mixed = _merge(y_a, y_b, w_pa[0].astype(BF16), w_pb[0].astype(BF16), proj, 6 * (d // PROJ_COLS), tm,
                   PROJ_COLS)
    h2, xn2t = _outproj(mixed, w_out[0].astype(BF16), xf, row(ln2_g[0]), _tile(n, OUT_ROWS))

    tp = _tile(n, PEER_TOKENS)
    r2, e2, m, w = _retrieve(xn2t, peer_wq[0].T.astype(BF16), peer_keys[0].astype(BF16), tp)
    out = _experts(xn2t, peer_u[0].astype(BF16), _transpose_cast(peer_v[0], CAST_ROWS), r2, e2, m, w, h2,
                   row(final_g), tp, PEER_EXPERTS)
    return out.reshape(batch, seq, d)
```
